```python
import math
import jax, jax.numpy as jnp
from jax import lax
import numpy as np

D_MODEL = 2048
BATCH = 8
SEQ = 2048
DEPTH = 2
DEC_BATCH = 2
DEC_SEQ = 4096
PAST_LEN = 128

MLA_HEADS = 16
Q_LORA = 512
KV_LORA = 512
QK_NOPE = 128
QK_ROPE = 64
V_HEAD = 128
ROPE_THETA = 10000.0
Q_BLOCK = 128
MLA_SCALE = (QK_NOPE + QK_ROPE) ** -0.5
SGU_CHUNK = 128
SGU_GROUPS = 4
SGU_WIDTH = D_MODEL
FNET_GROUPS = 4
FNET_WIDTH = D_MODEL
MEM_LEN = 256
X_HEADS = 4
X_HEAD_DIM = D_MODEL // X_HEADS
N_EXPERTS = 32
TOP_K = 4
D_FF = D_MODEL
SWIGLU_LIMIT = 7.0
SWIGLU_ALPHA = 1.702
N_BRANCH = 3
DEEPNORM_ALPHA = (2 * DEPTH) ** 0.25
DEEPNORM_BETA = (8 * DEPTH) ** -0.25
LN_EPS = 1e-5
RMS_EPS = 1e-6
OFF_Q = 0
OFF_KV = OFF_Q + Q_LORA
OFF_KR = OFF_KV + KV_LORA
OFF_SGU = OFF_KR + QK_ROPE
OFF_FNET = OFF_SGU + 2 * SGU_WIDTH
OFF_GATE = OFF_FNET + FNET_WIDTH
N_IN = OFF_GATE + N_BRANCH * D_MODEL

kernel_name = 'hybrid_mla_sgu_fnet_moe_encoder'


def layer_norm(x, g, b):
    xf = x.astype(jnp.float32)
    mu = jnp.mean(xf, axis=-1, keepdims=True)
    var = jnp.mean(jnp.square(xf - mu), axis=-1, keepdims=True)
    y = (xf - mu) * lax.rsqrt(var + LN_EPS) * g.astype(jnp.float32) + b.astype(jnp.float32)
    return y.astype(x.dtype)


def rms_norm(x, g):
    xf = x.astype(jnp.float32)
    y = xf * lax.rsqrt(jnp.mean(jnp.square(xf), axis=-1, keepdims=True) + RMS_EPS) * g.astype(jnp.float32)
    return y.astype(x.dtype)


def rope_tables(seq, dtype):
    inv = 1.0 / (ROPE_THETA ** (jnp.arange(0, QK_ROPE, 2, dtype=jnp.float32) / QK_ROPE))
    ang = jnp.arange(seq, dtype=jnp.float32)[:, None] * inv[None, :]
    return jnp.cos(ang).astype(dtype), jnp.sin(ang).astype(dtype)


def apply_rope(t, cos, sin):
    half = QK_ROPE // 2
    t1, t2 = t[..., :half], t[..., half:]
    return jnp.concatenate([t1 * cos - t2 * sin, t1 * sin + t2 * cos], axis=-1)


def mla_attention(q_nope, q_rope, k_nope, k_rope, v):
    B, S, H, _ = q_nope.shape
    nb = S // Q_BLOCK

    def to_blocks(t):
        return jnp.moveaxis(t.reshape((B, nb, Q_BLOCK) + t.shape[2:]), 1, 0)

    def one_block(blk):
        qn, qr = blk
        s = (jnp.einsum('bqhd,bkhd->bhqk', qn, k_nope, preferred_element_type=jnp.float32)
             + jnp.einsum('bqhr,bkr->bhqk', qr, k_rope, preferred_element_type=jnp.float32)) * MLA_SCALE
        p = jax.nn.softmax(s, axis=-1).astype(v.dtype)
        return jnp.einsum('bhqk,bkhd->bqhd', p, v)

    o = lax.map(one_block, (to_blocks(q_nope), to_blocks(q_rope)))
    return jnp.moveaxis(o, 0, 1).reshape(B, S, H * V_HEAD)


def mla_branch(z, q_norm, kv_norm, w_uq, w_ukv, w_o):
    B, S, _ = z.shape
    c_q = rms_norm(z[..., OFF_Q:OFF_KV], q_norm)
    c_kv = rms_norm(z[..., OFF_KV:OFF_KR], kv_norm)
    cos, sin = rope_tables(S, z.dtype)
    k_rope = apply_rope(z[..., OFF_KR:OFF_SGU], cos[None], sin[None])
    q = (c_q @ w_uq).reshape(B, S, MLA_HEADS, QK_NOPE + QK_ROPE)
    kv = (c_kv @ w_ukv).reshape(B, S, MLA_HEADS, QK_NOPE + V_HEAD)
    q_rope = apply_rope(q[..., QK_NOPE:], cos[None, :, None, :], sin[None, :, None, :])
    o = mla_attention(q[..., :QK_NOPE], q_rope, kv[..., :QK_NOPE], k_rope, kv[..., QK_NOPE:])
    return o @ w_o


def sgu_branch(z, ln_g, ln_b, ws, bs, w_o):
    B, S, _ = z.shape
    zs = jax.nn.gelu(z[..., OFF_SGU:OFF_FNET], approximate=False)
    u, v = zs[..., :SGU_WIDTH], zs[..., SGU_WIDTH:]
    v = layer_norm(v, ln_g, ln_b)
    vc = v.reshape(B, S // SGU_CHUNK, SGU_CHUNK, SGU_GROUPS, SGU_WIDTH // SGU_GROUPS)
    s = jnp.einsum('gpq,bnqgc->bnpgc', ws, vc) + bs.T[:, :, None]
    return (u * s.reshape(B, S, SGU_WIDTH)) @ w_o


def fnet_branch(z, w_o):
    B, S, _ = z.shape
    f = z[..., OFF_FNET:OFF_GATE].astype(jnp.float32).reshape(B, S, FNET_GROUPS, FNET_WIDTH // FNET_GROUPS)
    f = jnp.fft.fft2(f, axes=(1, 3), norm='ortho').real.astype(z.dtype).reshape(B, S, FNET_WIDTH)
    return f @ w_o


def cross_attention(x, mem, w_q, w_k, w_v, w_o):
    B, S, _ = x.shape
    M = mem.shape[1]
    q = (x @ w_q).reshape(B, S, X_HEADS, X_HEAD_DIM)
    k = (mem @ w_k).reshape(B, M, X_HEADS, X_HEAD_DIM)
    v = (mem @ w_v).reshape(B, M, X_HEADS, X_HEAD_DIM)
    s = jnp.einsum('bqhd,bkhd->bhqk', q, k, preferred_element_type=jnp.float32) * (X_HEAD_DIM ** -0.5)
    p = jax.nn.softmax(s, axis=-1).astype(v.dtype)
    o = jnp.einsum('bhqk,bkhd->bqhd', p, v).reshape(B, S, D_MODEL)
    return o @ w_o


def moe_ffn(x, w_router, b_router, w_gu, b_gu, w_down, b_down):
    B, S, D = x.shape
    h = x.reshape(B * S, D)
    logits = (h @ w_router + b_router).astype(jnp.float32)
    top_v, top_i = lax.top_k(logits, TOP_K)
    probs = jax.nn.softmax(top_v, axis=-1)
    combine = jnp.sum(jax.nn.one_hot(top_i, N_EXPERTS, dtype=jnp.float32) * probs[..., None], axis=1).astype(h.dtype)
    y = jnp.zeros_like(h)
    for e in range(N_EXPERTS):
        gu = h @ w_gu[e] + b_gu[e]
        gate = jnp.minimum(gu[:, :D_FF], SWIGLU_LIMIT)
        up = jnp.clip(gu[:, D_FF:], -SWIGLU_LIMIT, SWIGLU_LIMIT)
        act = (up + 1.0) * gate * jax.nn.sigmoid(SWIGLU_ALPHA * gate)
        y = y + combine[:, e:e + 1] * (act @ w_down[e] + b_down[e])
    return y.reshape(B, S, D)


def encoder_trunk(x, mem, w_in, b_gate, mla_q_norm, mla_kv_norm, w_uq, w_ukv, w_mla_o,
                  sgu_ln_g, sgu_ln_b, sgu_ws, sgu_bs, w_sgu_o, w_fnet_o, w_out, ln1_g, ln1_b,
                  w_cq, w_ck, w_cv, w_co, ln2_g, ln2_b,
                  w_router, b_router, w_gu, b_gu, w_down, b_down, ln3_g, ln3_b):
    B, S, D = x.shape
    for l in range(DEPTH):
        z = x @ w_in[l]
        y_a = mla_branch(z, mla_q_norm[l], mla_kv_norm[l], w_uq[l], w_ukv[l], w_mla_o[l])
        y_b = sgu_branch(z, sgu_ln_g[l], sgu_ln_b[l], sgu_ws[l], sgu_bs[l], w_sgu_o[l])
        y_c = fnet_branch(z, w_fnet_o[l])
        g = jax.nn.sigmoid(z[..., OFF_GATE:] + b_gate[l]).reshape(B, S, N_BRANCH, D)
        m = g[:, :, 0] * y_a + g[:, :, 1] * y_b + g[:, :, 2] * y_c
        x = layer_norm(DEEPNORM_ALPHA * x + m @ w_out[l], ln1_g[l], ln1_b[l])
        x = layer_norm(DEEPNORM_ALPHA * x + cross_attention(x, mem, w_cq[l], w_ck[l], w_cv[l], w_co[l]), ln2_g[l], ln2_b[l])
        x = layer_norm(DEEPNORM_ALPHA * x + moe_ffn(x, w_router[l], b_router[l], w_gu[l], b_gu[l], w_down[l], b_down[l]), ln3_g[l], ln3_b[l])
    return x


def setup_inputs(seed: int = 0) -> dict:
    key = jax.random.key(seed)
    ks = iter(jax.random.split(key, 40))
    L = DEPTH

    def nrm(shape, scale):
        return jax.random.normal(next(ks), shape, jnp.float32) * scale

    def gain(shape):
        return 1.0 + 0.01 * jax.random.normal(next(ks), shape, jnp.float32)

    return {
        'x_prompt': nrm((BATCH, SEQ, D_MODEL), 1.0),
        'x_sample': nrm((DEC_BATCH, DEC_SEQ, D_MODEL), 1.0),
        'mem_prompt': nrm((BATCH, MEM_LEN, D_MODEL), 1.0),
        'mem_sample': nrm((DEC_BATCH, MEM_LEN, D_MODEL), 1.0),
        'w_in': nrm((L, D_MODEL, N_IN), D_MODEL ** -0.5),
        'b_gate': nrm((L, N_BRANCH * D_MODEL), 0.01),
        'mla_q_norm': gain((L, Q_LORA)),
        'mla_kv_norm': gain((L, KV_LORA)),
        'w_uq': nrm((L, Q_LORA, MLA_HEADS * (QK_NOPE + QK_ROPE)), Q_LORA ** -0.5),
        'w_ukv': nrm((L, KV_LORA, MLA_HEADS * (QK_NOPE + V_HEAD)), KV_LORA ** -0.5),
        'w_mla_o': nrm((L, MLA_HEADS * V_HEAD, D_MODEL), (MLA_HEADS * V_HEAD) ** -0.5),
        'sgu_ln_g': gain((L, SGU_WIDTH)),
        'sgu_ln_b': nrm((L, SGU_WIDTH), 0.01),
        'sgu_ws': nrm((L, SGU_GROUPS, SGU_CHUNK, SGU_CHUNK), SGU_CHUNK ** -0.5),
        'sgu_bs': gain((L, SGU_GROUPS, SGU_CHUNK)),
        'w_sgu_o': nrm((L, SGU_WIDTH, D_MODEL), SGU_WIDTH ** -0.5),
        'w_fnet_o': nrm((L, FNET_WIDTH, D_MODEL), FNET_WIDTH ** -0.5),
        'w_out': nrm((L, D_MODEL, D_MODEL), DEEPNORM_BETA * D_MODEL ** -0.5),
        'ln1_g': gain((L, D_MODEL)),
        'ln1_b': nrm((L, D_MODEL), 0.01),
        'w_cq': nrm((L, D_MODEL, D_MODEL), D_MODEL ** -0.5),
        'w_ck': nrm((L, D_MODEL, D_MODEL), D_MODEL ** -0.5),
        'w_cv': nrm((L, D_MODEL, D_MODEL), D_MODEL ** -0.5),
        'w_co': nrm((L, D_MODEL, D_MODEL), DEEPNORM_BETA * D_MODEL ** -0.5),
        'ln2_g': gain((L, D_MODEL)),
        'ln2_b': nrm((L, D_MODEL), 0.01),
        'w_router': nrm((L, D_MODEL, N_EXPERTS), D_MODEL ** -0.5),
        'b_router': nrm((L, N_EXPERTS), 0.01),
        'w_gu': nrm((L, N_EXPERTS, D_MODEL, 2 * D_FF), D_MODEL ** -0.5),
        'b_gu': nrm((L, N_EXPERTS, 2 * D_FF), 0.01),
        'w_down': nrm((L, N_EXPERTS, D_FF, D_MODEL), DEEPNORM_BETA * D_FF ** -0.5),
        'b_down': nrm((L, N_EXPERTS, D_MODEL), 0.01),
        'ln3_g': gain((L, D_MODEL)),
        'ln3_b': nrm((L, D_MODEL), 0.01),
    }


def reference(x_prompt, x_sample, mem_prompt, mem_sample, w_in, b_gate, mla_q_norm, mla_kv_norm,
              w_uq, w_ukv, w_mla_o, sgu_ln_g, sgu_ln_b, sgu_ws, sgu_bs, w_sgu_o, w_fnet_o, w_out,
              ln1_g, ln1_b, w_cq, w_ck, w_cv, w_co, ln2_g, ln2_b,
              w_router, b_router, w_gu, b_gu, w_down, b_down, ln3_g, ln3_b):
    weights = (w_in, b_gate, mla_q_norm, mla_kv_norm, w_uq, w_ukv, w_mla_o,
               sgu_ln_g, sgu_ln_b, sgu_ws, sgu_bs, w_sgu_o, w_fnet_o, w_out, ln1_g, ln1_b,
               w_cq, w_ck, w_cv, w_co, ln2_g, ln2_b,
               w_router, b_router, w_gu, b_gu, w_down, b_down, ln3_g, ln3_b)
    y_prompt = encoder_trunk(x_prompt, mem_prompt, *weights)
    y_sample = encoder_trunk(x_sample, mem_sample, *weights)
    return (y_prompt, y_sample)
```

```python
import functools
import math

import jax
import jax.numpy as jnp
from jax import lax
from jax.experimental import pallas as pl
from jax.experimental.pallas import tpu as pltpu

D_MODEL = 2048
MLA_HEADS = 16
Q_LORA = 512
KV_LORA = 512
QK_NOPE = 128
QK_ROPE = 64
V_HEAD = 128
ROPE_THETA = 10000.0
MLA_SCALE = (QK_NOPE + QK_ROPE) ** -0.5
SGU_CHUNK = 128
SGU_GROUPS = 4
SGU_WIDTH = D_MODEL
FNET_GROUPS = 4
FNET_WIDTH = D_MODEL
FNET_GW = FNET_WIDTH // FNET_GROUPS
X_HEADS = 4
X_HEAD_DIM = D_MODEL // X_HEADS
N_EXPERTS = 32
TOP_K = 4
D_FF = D_MODEL
SWIGLU_LIMIT = 7.0
SWIGLU_ALPHA = 1.702
N_BRANCH = 3
LN_EPS = 1e-5
RMS_EPS = 1e-6
OFF_KV = Q_LORA
OFF_KR = OFF_KV + KV_LORA
OFF_SGU = OFF_KR + QK_ROPE
OFF_FNET = OFF_SGU + 2 * SGU_WIDTH
OFF_GATE = OFF_FNET + FNET_WIDTH

LANES = 128
Q_HEAD_PAD = 2 * LANES
LAT_PAD = OFF_SGU + (LANES - QK_ROPE)
VMEM_LIMIT = 56 * 1024 * 1024
NEG_BIG = -1e30

BF16 = jnp.bfloat16
F32 = jnp.float32


def _tile(n, pref):
    if n <= pref:
        return n
    t = pref
    while n % t:
        t //= 2
    return t


def _params(*sem):
    return pltpu.CompilerParams(dimension_semantics=sem, vmem_limit_bytes=VMEM_LIMIT)


def _dot(a, b):
    return jnp.dot(a, b, preferred_element_type=F32)


def _layer_norm(x, g, b):
    mu = jnp.mean(x, axis=-1, keepdims=True)
    xc = x - mu
    var = jnp.mean(xc * xc, axis=-1, keepdims=True)
    return xc * lax.rsqrt(var + LN_EPS) * g + b


def _rope_lanes(z, tab):
    return (z * tab[:, :LANES]
            + pltpu.roll(z, QK_ROPE // 2, axis=1) * tab[:, LANES:2 * LANES]
            + pltpu.roll(z, LANES - QK_ROPE // 2, axis=1) * tab[:, 2 * LANES:])


def _mm_kernel(*refs, n_extra, n_out, epilogue):
    a_ref, w_ref = refs[0], refs[1]
    extra = refs[2:2 + n_extra]
    outs = refs[2 + n_extra:2 + n_extra + n_out]
    acc = _dot(a_ref[...], w_ref[...])
    res = epilogue(acc, *[e[...] for e in extra])
    for o, r in zip(outs, res):
        o[...] = r.astype(o.dtype)


def _matmul(a, w, epilogue, out_dtypes, extras=(), tm=1024, tn=1024, name="mm"):
    m, k = a.shape
    n = w.shape[1]
    tm = _tile(m, tm)
    tn = _tile(n, tn)
    in_specs = [pl.BlockSpec((tm, k), lambda i, j: (i, 0)),
                pl.BlockSpec((k, tn), lambda i, j: (0, j))]
    args = [a, w]
    for kind, arr in extras:
        if kind == "col":
            in_specs.append(pl.BlockSpec((1, tn), lambda i, j: (0, j)))
        else:
            in_specs.append(pl.BlockSpec((tm, tn), lambda i, j: (i, j)))
        args.append(arr)
    out_shape = [jax.ShapeDtypeStruct((m, n), dt) for dt in out_dtypes]
    out_specs = [pl.BlockSpec((tm, tn), lambda i, j: (i, j)) for _ in out_dtypes]
    kern = functools.partial(_mm_kernel, n_extra=len(extras), n_out=len(out_dtypes), epilogue=epilogue)
    return pl.pallas_call(
        kern, grid=(m // tm, n // tn), in_specs=in_specs, out_specs=out_specs, out_shape=out_shape,
        compiler_params=_params("parallel", "arbitrary"), name=name)(*args)


def _ep_identity(acc):
    return (acc,)


def _ep_gelu(acc):
    return (0.5 * acc * (1.0 + lax.erf(acc * (2.0 ** -0.5))),)


def _ep_sigmoid_bias(acc, b):
    return (jax.nn.sigmoid(acc + b),)


def _ep_scale(acc, *, scale):
    return (acc * scale,)


def _latent_kernel(x_ref, w_ref, qg_ref, kvg_ref, tab_ref, cq_ref, ckv_ref, kr_ref):
    z = _dot(x_ref[...], w_ref[...])

    def rms(t, g):
        return t * lax.rsqrt(jnp.mean(t * t, axis=-1, keepdims=True) + RMS_EPS) * g

    cq_ref[...] = rms(z[:, :OFF_KV], qg_ref[...]).astype(BF16)
    ckv_ref[...] = rms(z[:, OFF_KV:OFF_KR], kvg_ref[...]).astype(BF16)
    kr_ref[...] = _rope_lanes(z[:, OFF_KR:], tab_ref[...]).astype(BF16)


def _latent(xb, w_lat, qg, kvg, tab):
    m, k = xb.shape
    tm = _tile(m, 512)
    return pl.pallas_call(
        _latent_kernel, grid=(m // tm,),
        in_specs=[pl.BlockSpec((tm, k), lambda i: (i, 0)),
                  pl.BlockSpec((k, LAT_PAD), lambda i: (0, 0)),
                  pl.BlockSpec((1, Q_LORA), lambda i: (0, 0)),
                  pl.BlockSpec((1, KV_LORA), lambda i: (0, 0)),
                  pl.BlockSpec((tm, 3 * LANES), lambda i: (i, 0))],
        out_specs=[pl.BlockSpec((tm, Q_LORA), lambda i: (i, 0)),
                   pl.BlockSpec((tm, KV_LORA), lambda i: (i, 0)),
                   pl.BlockSpec((tm, LANES), lambda i: (i, 0))],
        out_shape=[jax.ShapeDtypeStruct((m, Q_LORA), BF16),
                   jax.ShapeDtypeStruct((m, KV_LORA), BF16),
                   jax.ShapeDtypeStruct((m, LANES), BF16)],
        compiler_params=_params("parallel"), name="latent")(xb, w_lat, qg, kvg, tab)


def _qproj_kernel(c_ref, w_ref, tab_ref, q_ref, *, heads_per_tile):
    acc = _dot(c_ref[...], w_ref[...])
    tab = tab_ref[...]
    for h in range(heads_per_tile):
        lo = h * Q_HEAD_PAD
        q_ref[:, lo:lo + LANES] = (acc[:, lo:lo + LANES] * MLA_SCALE).astype(BF16)
        q_ref[:, lo + LANES:lo + Q_HEAD_PAD] = (
            _rope_lanes(acc[:, lo + LANES:lo + Q_HEAD_PAD], tab) * MLA_SCALE).astype(BF16)


def _qproj(cq, wq, tab):
    m, k = cq.shape
    n = wq.shape[1]
    tm = _tile(m, 1024)
    tn = 4 * Q_HEAD_PAD
    return pl.pallas_call(
        functools.partial(_qproj_kernel, heads_per_tile=tn // Q_HEAD_PAD),
        grid=(m // tm, n // tn),
        in_specs=[pl.BlockSpec((tm, k), lambda i, j: (i, 0)),
                  pl.BlockSpec((k, tn), lambda i, j: (0, j)),
                  pl.BlockSpec((tm, 3 * LANES), lambda i, j: (i, 0))],
        out_specs=pl.BlockSpec((tm, tn), lambda i, j: (i, j)),
        out_shape=jax.ShapeDtypeStruct((m, n), BF16),
        compiler_params=_params("parallel", "arbitrary"), name="qproj")(cq, wq, tab)


def _mla_kernel(q_ref, kn_ref, kr_ref, v_ref, o_ref, kcat_ref):
    @pl.when(pl.program_id(2) == 0)
    def _():
        kcat_ref[:, :LANES] = kn_ref[...]
        kcat_ref[:, LANES:] = kr_ref[...]

    s = lax.dot_general(q_ref[...], kcat_ref[...], (((1,), (1,)), ((), ())),
                        preferred_element_type=F32)
    p = jnp.exp(s - jnp.max(s, axis=-1, keepdims=True))
    l = jnp.sum(p, axis=-1, keepdims=True)
    o = _dot(p.astype(BF16), v_ref[...])
    o_ref[...] = (o / l).astype(BF16)


def _mla_attention(q, kv, kr, row0, nb, s):
    tq = _tile(s, 256)
    nq = s // tq
    qb0 = row0 // tq
    sb0 = row0 // s
    return pl.pallas_call(
        _mla_kernel, grid=(nb, MLA_HEADS, nq),
        in_specs=[pl.BlockSpec((tq, Q_HEAD_PAD), lambda b, h, i: (qb0 + b * nq + i, h)),
                  pl.BlockSpec((s, LANES), lambda b, h, i: (sb0 + b, 2 * h)),
                  pl.BlockSpec((s, LANES), lambda b, h, i: (sb0 + b, 0)),
                  pl.BlockSpec((s, LANES), lambda b, h, i: (sb0 + b, 2 * h + 1))],
        out_specs=pl.BlockSpec((tq, V_HEAD), lambda b, h, i: (b * nq + i, h)),
        out_shape=jax.ShapeDtypeStruct((nb * s, MLA_HEADS * V_HEAD), BF16),
        scratch_shapes=[pltpu.VMEM((s, Q_HEAD_PAD), BF16)],
        compiler_params=_params("parallel", "parallel", "arbitrary"), name="mla_attn")(q, kv, kr, kv)


def _sgu_kernel(u_ref, v_ref, g_ref, b_ref, ws_ref, bs_ref, o_ref, *, chunks):
    v = _layer_norm(v_ref[...].astype(F32), g_ref[...], b_ref[...]).astype(BF16)
    gw = SGU_WIDTH // SGU_GROUPS
    for c in range(chunks):
        r0 = c * SGU_CHUNK
        for g in range(SGU_GROUPS):
            sg = _dot(ws_ref[g], v[r0:r0 + SGU_CHUNK, g * gw:(g + 1) * gw])
            sg = sg + bs_ref[:, g * gw:(g + 1) * gw]
            u = u_ref[r0:r0 + SGU_CHUNK, g * gw:(g + 1) * gw].astype(F32)
            o_ref[r0:r0 + SGU_CHUNK, g * gw:(g + 1) * gw] = (u * sg).astype(BF16)


def _sgu(zs, ln_g, ln_b, ws, bs_full):
    m = zs.shape[0]
    tc = _tile(m, 2 * SGU_CHUNK)
    return pl.pallas_call(
        functools.partial(_sgu_kernel, chunks=tc // SGU_CHUNK), grid=(m // tc,),
        in_specs=[pl.BlockSpec((tc, SGU_WIDTH), lambda i: (i, 0)),
                  pl.BlockSpec((tc, SGU_WIDTH), lambda i: (i, 1)),
                  pl.BlockSpec((1, SGU_WIDTH), lambda i: (0, 0)),
                  pl.BlockSpec((1, SGU_WIDTH), lambda i: (0, 0)),
                  pl.BlockSpec((SGU_GROUPS, SGU_CHUNK, SGU_CHUNK), lambda i: (0, 0, 0)),
                  pl.BlockSpec((SGU_CHUNK, SGU_WIDTH), lambda i: (0, 0))],
        out_specs=pl.BlockSpec((tc, SGU_WIDTH), lambda i: (i, 0)),
        out_shape=jax.ShapeDtypeStruct((m, SGU_WIDTH), BF16),
        compiler_params=_params("parallel"), name="sgu")(zs, zs, ln_g, ln_b, ws, bs_full)


def _chan_dft_kernel(z_ref, w_ref, c_ref, s_ref):
    acc = _dot(z_ref[...], w_ref[...])
    c_ref[...] = acc[:, :FNET_GW].astype(BF16)
    s_ref[...] = acc[:, FNET_GW:].astype(BF16)


def _chan_dft(zf, w_cs):
    m = zf.shape[0]
    tm = _tile(m, 1024)
    blk = pl.BlockSpec((tm, FNET_GW), lambda i, g: (i, g))
    return pl.pallas_call(
        _chan_dft_kernel, grid=(m // tm, FNET_GROUPS),
        in_specs=[blk, pl.BlockSpec((FNET_GW, 2 * FNET_GW), lambda i, g: (0, 0))],
        out_specs=[blk, blk],
        out_shape=[jax.ShapeDtypeStruct((m, FNET_WIDTH), BF16)] * 2,
        compiler_params=_params("parallel", "arbitrary"), name="chan_dft")(zf, w_cs)


def _seq_dft_kernel(cs_ref, ss_ref, xc_ref, xs_ref, y_ref):
    y = _dot(cs_ref[...], xc_ref[...]) - _dot(ss_ref[...], xs_ref[...])
    y_ref[...] = y.astype(BF16)


def _seq_dft(cs, ss, xc, xs, row0, nb, s):
    tm = _tile(s, 512)
    tn = 512
    ni = s // tm
    sb0 = row0 // s
    return pl.pallas_call(
        _seq_dft_kernel, grid=(nb, FNET_WIDTH // tn, ni),
        in_specs=[pl.BlockSpec((tm, s), lambda b, j, i: (i, 0)),
                  pl.BlockSpec((tm, s), lambda b, j, i: (i, 0)),
                  pl.BlockSpec((s, tn), lambda b, j, i: (sb0 + b, j)),
                  pl.BlockSpec((s, tn), lambda b, j, i: (sb0 + b, j))],
        out_specs=pl.BlockSpec((tm, tn), lambda b, j, i: (b * ni + i, j)),
        out_shape=jax.ShapeDtypeStruct((nb * s, FNET_WIDTH), BF16),
        compiler_params=_params("parallel", "parallel", "arbitrary"), name="seq_dft")(cs, ss, xc, xs)


def _dft_tables(n):
    j = lax.iota(jnp.int32, n)
    jk = (j[:, None] * j[None, :]) % n
    ang = jk.astype(F32) * (2.0 * math.pi / n)
    sc = n ** -0.5
    return (jnp.cos(ang) * sc).astype(BF16), (jnp.sin(ang) * sc).astype(BF16)


def _branch_kernel(a0, a1, a2, w0, w1, w2, g0, g1, g2, m_ref):
    m = (g0[...].astype(F32) * _dot(a0[...], w0[...])
         + g1[...].astype(F32) * _dot(a1[...], w1[...])
         + g2[...].astype(F32) * _dot(a2[...], w2[...]))
    m_ref[...] = m.astype(BF16)


def _branch_merge(a0, a1, a2, w0, w1, w2, gates):
    m = a0.shape[0]
    tm = _tile(m, 512)
    tn = 512
    nb = D_MODEL // tn
    a_spec = pl.BlockSpec((tm, D_MODEL), lambda i, j: (i, 0))
    w_spec = pl.BlockSpec((D_MODEL, tn), lambda i, j: (0, j))
    g_specs = [pl.BlockSpec((tm, tn), functools.partial(lambda i, j, r: (i, r * nb + j), r=r))
               for r in range(N_BRANCH)]
    return pl.pallas_call(
        _branch_kernel, grid=(m // tm, nb),
        in_specs=[a_spec] * 3 + [w_spec] * 3 + g_specs,
        out_specs=pl.BlockSpec((tm, tn), lambda i, j: (i, j)),
        out_shape=jax.ShapeDtypeStruct((m, D_MODEL), BF16),
        compiler_params=_params("parallel", "arbitrary"), name="branch_merge")(
            a0, a1, a2, w0, w1, w2, gates, gates, gates)


def _proj_ln_kernel(a_ref, w_ref, x_ref, g_ref, b_ref, xo_ref, xb_ref, *, alpha):
    y = alpha * x_ref[...] + _dot(a_ref[...], w_ref[...])
    xn = _layer_norm(y, g_ref[...], b_ref[...])
    xo_ref[...] = xn
    xb_ref[...] = xn.astype(BF16)


def _proj_ln(a, w, x, g, b, alpha):
    m = a.shape[0]
    tm = _tile(m, 512)
    row = pl.BlockSpec((tm, D_MODEL), lambda i: (i, 0))
    vec = pl.BlockSpec((1, D_MODEL), lambda i: (0, 0))
    return pl.pallas_call(
        functools.partial(_proj_ln_kernel, alpha=alpha), grid=(m // tm,),
        in_specs=[row, pl.BlockSpec((D_MODEL, D_MODEL), lambda i: (0, 0)), row, vec, vec],
        out_specs=[row, row],
        out_shape=[jax.ShapeDtypeStruct((m, D_MODEL), F32), jax.ShapeDtypeStruct((m, D_MODEL), BF16)],
        compiler_params=_params("parallel"), name="proj_ln")(a, w, x, g, b)


def _xattn_kernel(q_ref, k_ref, v_ref, o_ref):
    for h in range(X_HEADS):
        sl = slice(h * X_HEAD_DIM, (h + 1) * X_HEAD_DIM)
        s = lax.dot_general(q_ref[:, sl], k_ref[:, sl], (((1,), (1,)), ((), ())),
                            preferred_element_type=F32)
        p = jnp.exp(s - jnp.max(s, axis=-1, keepdims=True))
        l = jnp.sum(p, axis=-1, keepdims=True)
        o_ref[:, sl] = (_dot(p.astype(BF16), v_ref[:, sl]) / l).astype(BF16)


def _xattn(q, km, vm, mem_len, groups):
    m = q.shape[0]
    tq = _tile(min(s for _, s in groups), 512)
    bounds = []
    t0 = r0 = 0
    for rows, s in groups:
        bounds.append((t0, s // tq, r0))
        t0 += rows // tq
        r0 += rows // s

    def mem_block(i):
        blk = jnp.int32(0)
        for first, per, req0 in bounds:
            blk = jnp.where(i >= first, req0 + (i - first) // per, blk)
        return blk

    row = pl.BlockSpec((tq, D_MODEL), lambda i: (i, 0))
    mem = pl.BlockSpec((mem_len, D_MODEL), lambda i: (mem_block(i), 0))
    return pl.pallas_call(
        _xattn_kernel, grid=(m // tq,), in_specs=[row, mem, mem], out_specs=row,
        out_shape=jax.ShapeDtypeStruct((m, D_MODEL), BF16),
        compiler_params=_params("parallel"), name="xattn")(q, km, vm)


def _router_kernel(x_ref, w_ref, b_ref, ir_ref, p_ref, cnt_ref, carry_ref):
    i = pl.program_id(0)

    @pl.when(i == 0)
    def _():
        carry_ref[...] = jnp.zeros_like(carry_ref)

    tm = x_ref.shape[0]
    logits = jnp.dot(x_ref[...], w_ref[...], preferred_element_type=F32,
                     precision=lax.Precision.HIGHEST) + b_ref[...]
    lane = lax.broadcasted_iota(jnp.int32, (tm, LANES), 1)
    tops, idxs = [], []
    cur = logits
    for _ in range(TOP_K):
        mx = jnp.max(cur, axis=-1, keepdims=True)
        ix = jnp.min(jnp.where(cur == mx, lane, LANES), axis=-1, keepdims=True)
        tops.append(mx)
        idxs.append(ix)
        cur = jnp.where(lane == ix, NEG_BIG * 2.0, cur)
    exps = [jnp.exp(t - tops[0]) for t in tops]
    denom = exps[0] + exps[1] + exps[2] + exps[3]

    onehot = jnp.zeros((tm, LANES), F32)
    for ix in idxs:
        onehot = onehot + (lane == ix).astype(F32)
    r = lax.broadcasted_iota(jnp.int32, (tm, tm), 0)
    c = lax.broadcasted_iota(jnp.int32, (tm, tm), 1)
    lower = (c < r).astype(BF16)
    before = _dot(lower, onehot.astype(BF16)) + carry_ref[...]

    ir = jnp.zeros((tm, LANES), jnp.int32)
    pr = jnp.zeros((tm, LANES), F32)
    for k in range(TOP_K):
        rank = jnp.sum(jnp.where(lane == idxs[k], before, 0.0), axis=-1, keepdims=True)
        ir = jnp.where(lane == k, idxs[k], ir)
        ir = jnp.where(lane == TOP_K + k, rank.astype(jnp.int32), ir)
        pr = jnp.where(lane == k, exps[k] / denom, pr)
    ir_ref[...] = ir
    p_ref[...] = pr
    carry_ref[...] = carry_ref[...] + jnp.sum(onehot, axis=0, keepdims=True)
    cnt_ref[...] = carry_ref[...]


def _router(x, w_pad, b_pad):
    m = x.shape[0]
    tm = _tile(m, 512)
    return pl.pallas_call(
        _router_kernel, grid=(m // tm,),
        in_specs=[pl.BlockSpec((tm, D_MODEL), lambda i: (i, 0)),
                  pl.BlockSpec((D_MODEL, LANES), lambda i: (0, 0)),
                  pl.BlockSpec((1, LANES), lambda i: (0, 0))],
        out_specs=[pl.BlockSpec((tm, LANES), lambda i: (i, 0)),
                   pl.BlockSpec((tm, LANES), lambda i: (i, 0)),
                   pl.BlockSpec((1, LANES), lambda i: (0, 0))],
        out_shape=[jax.ShapeDtypeStruct((m, LANES), jnp.int32),
                   jax.ShapeDtypeStruct((m, LANES), F32),
                   jax.ShapeDtypeStruct((1, LANES), F32)],
        scratch_shapes=[pltpu.VMEM((1, LANES), F32)],
        compiler_params=_params("arbitrary"), name="router")(x, w_pad, b_pad)


def _moe_kernel(te_ref, nu_ref, idx_cur, idx_nxt, x_hbm, wg_ref, wu_ref, bg_ref, bu_ref, wd_ref, bd_ref,
                out_ref, xbuf, xb_ref, acc_ref, sem, *, tm, nf):
    j = pl.program_id(0)
    f = pl.program_id(1)
    n_used = nu_ref[0]

    def start_gather(idx_ref, slot):
        def body(r, carry):
            t = idx_ref[0, 0, r]
            pltpu.make_async_copy(x_hbm.at[pl.ds(t, 1), :], xbuf.at[slot, pl.ds(r, 1), :],
                                  sem.at[slot]).start()
            return carry
        lax.fori_loop(0, tm, body, 0, unroll=8)

    @pl.when(jnp.logical_and(j == 0, f == 0))
    def _():
        start_gather(idx_cur, 0)

    @pl.when(jnp.logical_and(f == 0, j + 1 < n_used))
    def _():
        start_gather(idx_nxt, (j + 1) % 2)

    @pl.when(j < n_used)
    def _():
        slot = j % 2

        @pl.when(f == 0)
        def _():
            pltpu.make_async_copy(x_hbm.at[pl.ds(0, tm), :], xbuf.at[slot], sem.at[slot]).wait()
            xb_ref[...] = xbuf[slot].astype(BF16)

        x = xb_ref[...]
        gate = jnp.minimum(_dot(x, wg_ref[0]) + bg_ref[0], SWIGLU_LIMIT)
        up = jnp.clip(_dot(x, wu_ref[0]) + bu_ref[0], -SWIGLU_LIMIT, SWIGLU_LIMIT)
        act = (up + 1.0) * gate * jax.nn.sigmoid(SWIGLU_ALPHA * gate)
        part = _dot(act.astype(BF16), wd_ref[0])

        @pl.when(f == 0)
        def _():
            acc_ref[...] = part

        @pl.when(f > 0)
        def _():
            acc_ref[...] += part

        @pl.when(f == nf - 1)
        def _():
            out_ref[...] = acc_ref[...] + bd_ref[0]

    @pl.when(jnp.logical_and(j >= n_used, f == nf - 1))
    def _():
        out_ref[...] = jnp.zeros_like(out_ref)


def _moe_experts(tile_expert, n_used, src_tok, x, w_gu, b_gu, w_down, b_down, tm):
    n_tiles = tile_expert.shape[0]
    tf = 512
    nf = D_FF // tf
    idx3 = src_tok.reshape(n_tiles, 1, tm)

    def fchunk(j, f, nu):
        return jnp.where(j < nu[0], f, nf - 1)

    grid_spec = pltpu.PrefetchScalarGridSpec(
        num_scalar_prefetch=2, grid=(n_tiles, nf),
        in_specs=[
            pl.BlockSpec((1, 1, tm), lambda j, f, te, nu: (j, 0, 0), memory_space=pltpu.SMEM),
            pl.BlockSpec((1, 1, tm), lambda j, f, te, nu: (jnp.minimum(j + 1, n_tiles - 1), 0, 0),
                         memory_space=pltpu.SMEM),
            pl.BlockSpec(memory_space=pl.ANY),
            pl.BlockSpec((1, D_MODEL, tf), lambda j, f, te, nu: (te[j], 0, fchunk(j, f, nu))),
            pl.BlockSpec((1, D_MODEL, tf), lambda j, f, te, nu: (te[j], 0, nf + fchunk(j, f, nu))),
            pl.BlockSpec((1, 1, tf), lambda j, f, te, nu: (te[j], 0, fchunk(j, f, nu))),
            pl.BlockSpec((1, 1, tf), lambda j, f, te, nu: (te[j], 0, nf + fchunk(j, f, nu))),
            pl.BlockSpec((1, tf, D_MODEL), lambda j, f, te, nu: (te[j], fchunk(j, f, nu), 0)),
            pl.BlockSpec((1, 1, D_MODEL), lambda j, f, te, nu: (te[j], 0, 0)),
        ],
        out_specs=pl.BlockSpec((tm, D_MODEL), lambda j, f, te, nu: (j, 0)),
        scratch_shapes=[pltpu.VMEM((2, tm, D_MODEL), F32),
                        pltpu.VMEM((tm, D_MODEL), BF16),
                        pltpu.VMEM((tm, D_MODEL), F32),
                        pltpu.SemaphoreType.DMA((2,))])
    return pl.pallas_call(
        functools.partial(_moe_kernel, tm=tm, nf=nf), grid_spec=grid_spec,
        out_shape=jax.ShapeDtypeStruct((n_tiles * tm, D_MODEL), F32),
        compiler_params=_params("arbitrary", "arbitrary"), name="moe_experts")(
            tile_expert, n_used, idx3, idx3, x, w_gu, w_gu,
            b_gu.reshape(N_EXPERTS, 1, 2 * D_FF), b_gu.reshape(N_EXPERTS, 1, 2 * D_FF),
            w_down, b_down.reshape(N_EXPERTS, 1, D_MODEL))


def _combine_kernel(pos_cur, pos_nxt, ys_hbm, p_ref, x_ref, g_ref, b_ref, xo_ref, xb_ref, buf, sem,
                    *, tm, alpha):
    i = pl.program_id(0)
    n = pl.num_programs(0)

    def start_gather(pos_ref, slot):
        def body(r, carry):
            for k in range(TOP_K):
                t = pos_ref[0, 0, r * TOP_K + k]
                pltpu.make_async_copy(ys_hbm.at[pl.ds(t, 1), :], buf.at[slot, k, pl.ds(r, 1), :],
                                      sem.at[slot]).start()
            return carry
        lax.fori_loop(0, tm, body, 0, unroll=2)

    @pl.when(i == 0)
    def _():
        start_gather(pos_cur, 0)

    @pl.when(i + 1 < n)
    def _():
        start_gather(pos_nxt, (i + 1) % 2)

    slot = i % 2
    for k in range(TOP_K):
        pltpu.make_async_copy(ys_hbm.at[pl.ds(0, tm), :], buf.at[slot, k], sem.at[slot]).wait()
    p = p_ref[...]
    y = alpha * x_ref[...]
    for k in range(TOP_K):
        y = y + p[:, k:k + 1] * buf[slot, k]
    xn = _layer_norm(y, g_ref[...], b_ref[...])
    xo_ref[...] = xn
    xb_ref[...] = xn.astype(BF16)


def _combine_ln(pos, ys, probs, x, g, b, alpha):
    m = x.shape[0]
    tm = _tile(m, 256)
    nt = m // tm
    pos3 = pos.reshape(nt, 1, tm * TOP_K)
    row = pl.BlockSpec((tm, D_MODEL), lambda i: (i, 0))
    vec = pl.BlockSpec((1, D_MODEL), lambda i: (0, 0))
    return pl.pallas_call(
        functools.partial(_combine_kernel, tm=tm, alpha=alpha), grid=(nt,),
        in_specs=[pl.BlockSpec((1, 1, tm * TOP_K), lambda i: (i, 0, 0), memory_space=pltpu.SMEM),
                  pl.BlockSpec((1, 1, tm * TOP_K), lambda i: (jnp.minimum(i + 1, nt - 1), 0, 0),
                               memory_space=pltpu.SMEM),
                  pl.BlockSpec(memory_space=pl.ANY),
                  pl.BlockSpec((tm, LANES), lambda i: (i, 0)),
                  row, vec, vec],
        out_specs=[row, row],
        out_shape=[jax.ShapeDtypeStruct((m, D_MODEL), F32), jax.ShapeDtypeStruct((m, D_MODEL), BF16)],
        scratch_shapes=[pltpu.VMEM((2, TOP_K, tm, D_MODEL), F32), pltpu.SemaphoreType.DMA((2,))],
        compiler_params=_params("arbitrary"), name="combine_ln")(pos3, pos3, ys, probs, x, g, b)


def _routing_plan(ir, counts, tm):
    m = ir.shape[0]
    n_tiles = (m * TOP_K) // tm + N_EXPERTS
    idx = ir[:, :TOP_K]
    rank = ir[:, TOP_K:2 * TOP_K]
    cnt = counts[0, :N_EXPERTS].astype(jnp.int32)
    tiles_per = (cnt + tm - 1) // tm
    tile_end = jnp.cumsum(tiles_per)
    tile_start = tile_end - tiles_per
    n_used = tile_end[-1]
    pos = tile_start[idx] * tm + rank
    tile_ids = jnp.arange(n_tiles, dtype=jnp.int32)
    tile_expert = jnp.searchsorted(tile_end, jnp.minimum(tile_ids, n_used - 1), side="right")
    tile_expert = jnp.minimum(tile_expert, N_EXPERTS - 1).astype(jnp.int32)
    tok = jnp.broadcast_to(jnp.arange(m, dtype=jnp.int32)[:, None], (m, TOP_K))
    src_tok = jnp.zeros((n_tiles * tm,), jnp.int32).at[pos.reshape(-1)].set(tok.reshape(-1))
    return pos.astype(jnp.int32), tile_expert, n_used.reshape(1).astype(jnp.int32), src_tok


def _rope_table(groups):
    inv = 1.0 / (ROPE_THETA ** (jnp.arange(0, QK_ROPE, 2, dtype=F32) / QK_ROPE))
    pos = jnp.concatenate([jnp.tile(jnp.arange(s, dtype=F32), rows // s) for rows, s in groups])
    ang = pos[:, None] * inv[None, :]
    cos, sin = jnp.cos(ang), jnp.sin(ang)
    z = jnp.zeros_like(cos)
    return jnp.concatenate([cos, cos, z, z, z, sin, z, z, -sin, z, z, z], axis=1)


def kernel(x_prompt, x_sample, mem_prompt, mem_sample, w_in, b_gate, mla_q_norm, mla_kv_norm, w_uq, w_ukv,
           w_mla_o, sgu_ln_g, sgu_ln_b, sgu_ws, sgu_bs, w_sgu_o, w_fnet_o, w_out, ln1_g, ln1_b, w_cq, w_ck,
           w_cv, w_co, ln2_g, ln2_b, w_router, b_router, w_gu, b_gu, w_down, b_down, ln3_g, ln3_b):
    depth = w_in.shape[0]
    alpha = (2 * depth) ** 0.25
    bp, sp, _ = x_prompt.shape
    bs_, ss, _ = x_sample.shape
    mem_len = mem_prompt.shape[1]
    groups = ((bp * sp, sp), (bs_ * ss, ss))
    assert (bp * sp) % ss == 0 and sp % SGU_CHUNK == 0 and ss % SGU_CHUNK == 0
    m = bp * sp + bs_ * ss

    x = jnp.concatenate([x_prompt.reshape(bp * sp, D_MODEL), x_sample.reshape(bs_ * ss, D_MODEL)], axis=0)
    xb = x.astype(BF16)
    memb = jnp.concatenate([mem_prompt.reshape(bp * mem_len, D_MODEL),
                            mem_sample.reshape(bs_ * mem_len, D_MODEL)], axis=0).astype(BF16)

    tab = _rope_table(groups)
    dft_c, dft_s = _dft_tables(FNET_GW)
    w_chan = jnp.concatenate([dft_c, dft_s], axis=1)
    seq_tabs = {s: _dft_tables(s) for s in {sp, ss}}
    moe_tm = _tile(m * TOP_K, 512)

    for l in range(depth):
        w_lat = jnp.pad(w_in[l, :, :OFF_SGU], ((0, 0), (0, LAT_PAD - OFF_SGU))).astype(BF16)
        w_sgu_in = w_in[l, :, OFF_SGU:OFF_FNET].astype(BF16)
        w_fnet_in = w_in[l, :, OFF_FNET:OFF_GATE].astype(BF16)
        w_gate_in = w_in[l, :, OFF_GATE:].astype(BF16)
        wq = jnp.pad(w_uq[l].reshape(Q_LORA, MLA_HEADS, QK_NOPE + QK_ROPE),
                     ((0, 0), (0, 0), (0, Q_HEAD_PAD - QK_NOPE - QK_ROPE))
                     ).reshape(Q_LORA, MLA_HEADS * Q_HEAD_PAD).astype(BF16)
        bs_full = jnp.repeat(sgu_bs[l].T, SGU_WIDTH // SGU_GROUPS, axis=1)

        cq, ckv, kr = _latent(xb, w_lat, mla_q_norm[l][None], mla_kv_norm[l][None], tab)
        q = _qproj(cq, wq, tab)
        (kv,) = _matmul(ckv, w_ukv[l].astype(BF16), _ep_identity, [BF16], name="kvproj")
        o_parts, row0 = [], 0
        for rows, s in groups:
            o_parts.append(_mla_attention(q, kv, kr, row0, rows // s, s))
            row0 += rows
        o_mla = jnp.concatenate(o_parts, axis=0)

        (zs,) = _matmul(xb, w_sgu_in, _ep_gelu, [BF16], name="sgu_in")
        us = _sgu(zs, sgu_ln_g[l][None], sgu_ln_b[l][None], sgu_ws[l].astype(BF16), bs_full)

        (zf,) = _matmul(xb, w_fnet_in, _ep_identity, [BF16], name="fnet_in")
        xc, xs = _chan_dft(zf, w_chan)
        f_parts, row0 = [], 0
        for rows, s in groups:
            cs, sn = seq_tabs[s]
            f_parts.append(_seq_dft(cs, sn, xc, xs, row0, rows // s, s))
            row0 += rows
        fy = jnp.concatenate(f_parts, axis=0)

        (gates,) = _matmul(xb, w_gate_in, _ep_sigmoid_bias, [BF16], extras=[("col", b_gate[l][None])],
                           name="gates")
        mrg = _branch_merge(o_mla, us, fy, w_mla_o[l].astype(BF16), w_sgu_o[l].astype(BF16),
                            w_fnet_o[l].astype(BF16), gates)
        x, xb = _proj_ln(mrg, w_out[l].astype(BF16), x, ln1_g[l][None], ln1_b[l][None], alpha)

        (qx,) = _matmul(xb, w_cq[l].astype(BF16), functools.partial(_ep_scale, scale=X_HEAD_DIM ** -0.5),
                        [BF16], name="xq")
        (km,) = _matmul(memb, w_ck[l].astype(BF16), _ep_identity, [BF16], name="xk")
        (vm,) = _matmul(memb, w_cv[l].astype(BF16), _ep_identity, [BF16], name="xv")
        ox = _xattn(qx, km, vm, mem_len, groups)
        x, xb = _proj_ln(ox, w_co[l].astype(BF16), x, ln2_g[l][None], ln2_b[l][None], alpha)

        w_r = jnp.pad(w_router[l], ((0, 0), (0, LANES - N_EXPERTS)))
        b_r = jnp.pad(b_router[l], (0, LANES - N_EXPERTS), constant_values=NEG_BIG)[None]
        ir, probs, counts = _router(x, w_r, b_r)
        pos, tile_expert, n_used, src_tok = _routing_plan(ir, counts, moe_tm)
        ys = _moe_experts(tile_expert, n_used, src_tok, x, w_gu[l].astype(BF16), b_gu[l],
                          w_down[l].astype(BF16), b_down[l], moe_tm)
        x, xb = _combine_ln(pos, ys, probs, x, ln3_g[l][None], ln3_b[l][None], alpha)

    y_prompt = x[:bp * sp].reshape(bp, sp, D_MODEL)
    y_sample = x[bp * sp:].reshape(bs_, ss, D_MODEL)
    return (y_prompt, y_sample)
```

```python
import functools
import math

import jax
import jax.numpy as jnp
from jax import lax
from jax.experimental import pallas as pl
from jax.experimental.pallas import tpu as pltpu

D_MODEL = 2048
MLA_HEADS = 16
Q_LORA = 512
KV_LORA = 512
QK_NOPE = 128
QK_ROPE = 64
V_HEAD = 128
ROPE_THETA = 10000.0
MLA_SCALE = (QK_NOPE + QK_ROPE) ** -0.5
Q_PRESCALE = MLA_SCALE * math.log2(math.e)
SGU_CHUNK = 128
SGU_GROUPS = 4
SGU_WIDTH = D_MODEL
FNET_GROUPS = 4
FNET_WIDTH = D_MODEL
FNET_GW = FNET_WIDTH // FNET_GROUPS
X_HEADS = 4
X_HEAD_DIM = D_MODEL // X_HEADS
N_EXPERTS = 32
TOP_K = 4
D_FF = D_MODEL
SWIGLU_LIMIT = 7.0
SWIGLU_ALPHA = 1.702
N_BRANCH = 3
LN_EPS = 1e-5
RMS_EPS = 1e-6
OFF_KV = Q_LORA
OFF_KR = OFF_KV + KV_LORA
OFF_SGU = OFF_KR + QK_ROPE
OFF_FNET = OFF_SGU + 2 * SGU_WIDTH
OFF_GATE = OFF_FNET + FNET_WIDTH

LANES = 128
SUBLANES = 8
Q_HEAD_PAD = 2 * LANES
LAT_PAD = OFF_SGU + (LANES - QK_ROPE)
VMEM_LIMIT = 56 * 1024 * 1024
NEG_BIG = -1e30

BF16 = jnp.bfloat16
F32 = jnp.float32


def _tile(n, pref):
    if n <= pref:
        return n
    t = pref
    while n % t:
        t //= 2
    return t


def _params(*sem):
    return pltpu.CompilerParams(dimension_semantics=sem, vmem_limit_bytes=VMEM_LIMIT)


def _dot(a, b):
    return jnp.dot(a, b, preferred_element_type=F32)


def _layer_norm(x, g, b):
    mu = jnp.mean(x, axis=-1, keepdims=True)
    xc = x - mu
    var = jnp.mean(xc * xc, axis=-1, keepdims=True)
    return xc * lax.rsqrt(var + LN_EPS) * g + b


def _rope_lanes(z, tab):
    return (z * tab[:, :LANES]
            + pltpu.roll(z, QK_ROPE // 2, axis=1) * tab[:, LANES:2 * LANES]
            + pltpu.roll(z, LANES - QK_ROPE // 2, axis=1) * tab[:, 2 * LANES:])


def _mm_kernel(*refs, n_extra, n_out, epilogue):
    a_ref, w_ref = refs[0], refs[1]
    extra = refs[2:2 + n_extra]
    outs = refs[2 + n_extra:2 + n_extra + n_out]
    acc = _dot(a_ref[...], w_ref[...])
    res = epilogue(acc, *[e[...] for e in extra])
    for o, r in zip(outs, res):
        o[...] = r.astype(o.dtype)


def _matmul(a, w, epilogue, out_dtypes, extras=(), tm=1024, tn=1024, name="mm"):
    m, k = a.shape
    n = w.shape[1]
    tm = _tile(m, tm)
    tn = _tile(n, tn)
    in_specs = [pl.BlockSpec((tm, k), lambda i, j: (i, 0)),
                pl.BlockSpec((k, tn), lambda i, j: (0, j))]
    args = [a, w]
    for kind, arr in extras:
        if kind == "col":
            in_specs.append(pl.BlockSpec((1, tn), lambda i, j: (0, j)))
        else:
            in_specs.append(pl.BlockSpec((tm, tn), lambda i, j: (i, j)))
        args.append(arr)
    out_shape = [jax.ShapeDtypeStruct((m, n), dt) for dt in out_dtypes]
    out_specs = [pl.BlockSpec((tm, tn), lambda i, j: (i, j)) for _ in out_dtypes]
    kern = functools.partial(_mm_kernel, n_extra=len(extras), n_out=len(out_dtypes), epilogue=epilogue)
    return pl.pallas_call(
        kern, grid=(m // tm, n // tn), in_specs=in_specs, out_specs=out_specs, out_shape=out_shape,
        compiler_params=_params("parallel", "arbitrary"), name=name)(*args)


def _ep_identity(acc):
    return (acc,)


def _ep_gelu(acc):
    return (0.5 * acc * (1.0 + lax.erf(acc * (2.0 ** -0.5))),)


def _ep_sigmoid_bias(acc, b):
    return (jax.nn.sigmoid(acc + b),)


def _ep_scale(acc, *, scale):
    return (acc * scale,)


def _latent_kernel(x_ref, w_ref, qg_ref, kvg_ref, tab_ref, cq_ref, ckv_ref, kr_ref):
    z = _dot(x_ref[...], w_ref[...])

    def rms(t, g):
        return t * lax.rsqrt(jnp.mean(t * t, axis=-1, keepdims=True) + RMS_EPS) * g

    cq_ref[...] = rms(z[:, :OFF_KV], qg_ref[...]).astype(BF16)
    ckv_ref[...] = rms(z[:, OFF_KV:OFF_KR], kvg_ref[...]).astype(BF16)
    kr_ref[...] = _rope_lanes(z[:, OFF_KR:], tab_ref[...]).astype(BF16)


def _latent(xb, w_lat, qg, kvg, tab):
    m, k = xb.shape
    tm = _tile(m, 512)
    return pl.pallas_call(
        _latent_kernel, grid=(m // tm,),
        in_specs=[pl.BlockSpec((tm, k), lambda i: (i, 0)),
                  pl.BlockSpec((k, LAT_PAD), lambda i: (0, 0)),
                  pl.BlockSpec((1, Q_LORA), lambda i: (0, 0)),
                  pl.BlockSpec((1, KV_LORA), lambda i: (0, 0)),
                  pl.BlockSpec((tm, 3 * LANES), lambda i: (i, 0))],
        out_specs=[pl.BlockSpec((tm, Q_LORA), lambda i: (i, 0)),
                   pl.BlockSpec((tm, KV_LORA), lambda i: (i, 0)),
                   pl.BlockSpec((tm, LANES), lambda i: (i, 0))],
        out_shape=[jax.ShapeDtypeStruct((m, Q_LORA), BF16),
                   jax.ShapeDtypeStruct((m, KV_LORA), BF16),
                   jax.ShapeDtypeStruct((m, LANES), BF16)],
        compiler_params=_params("parallel"), name="latent")(xb, w_lat, qg, kvg, tab)


def _qproj_kernel(c_ref, w_ref, tab_ref, q_ref, *, heads_per_tile):
    acc = _dot(c_ref[...], w_ref[...])
    tab = tab_ref[...]
    for h in range(heads_per_tile):
        lo = h * Q_HEAD_PAD
        q_ref[:, lo:lo + LANES] = (acc[:, lo:lo + LANES] * Q_PRESCALE).astype(BF16)
        q_ref[:, lo + LANES:lo + Q_HEAD_PAD] = (
            _rope_lanes(acc[:, lo + LANES:lo + Q_HEAD_PAD], tab) * Q_PRESCALE).astype(BF16)


def _qproj(cq, wq, tab):
    m, k = cq.shape
    n = wq.shape[1]
    tm = _tile(m, 1024)
    tn = 4 * Q_HEAD_PAD
    return pl.pallas_call(
        functools.partial(_qproj_kernel, heads_per_tile=tn // Q_HEAD_PAD),
        grid=(m // tm, n // tn),
        in_specs=[pl.BlockSpec((tm, k), lambda i, j: (i, 0)),
                  pl.BlockSpec((k, tn), lambda i, j: (0, j)),
                  pl.BlockSpec((tm, 3 * LANES), lambda i, j: (i, 0))],
        out_specs=pl.BlockSpec((tm, tn), lambda i, j: (i, j)),
        out_shape=jax.ShapeDtypeStruct((m, n), BF16),
        compiler_params=_params("parallel", "arbitrary"), name="qproj")(cq, wq, tab)


MLA_HEADS_PER_STEP = 4
MLA_SCORE_SLOTS = 2
MLA_KEY_CHUNK = 512


def _mla_kernel(q_ref, kv_ref, kr_ref, o_ref, kcat_ref, s_ref, *, kc):
    seq = kv_ref.shape[0]
    tq = q_ref.shape[0]
    nck = seq // kc
    nt = (((1,), (1,)), ((), ()))

    @pl.when(pl.program_id(2) == 0)
    def _():
        for h in range(MLA_HEADS_PER_STEP):
            kcat_ref[h, :, :LANES] = kv_ref[:, h * Q_HEAD_PAD:h * Q_HEAD_PAD + LANES]
            kcat_ref[h, :, LANES:] = kr_ref[...]

    for h in range(MLA_HEADS_PER_STEP):
        q = q_ref[:, h * Q_HEAD_PAD:(h + 1) * Q_HEAD_PAD]
        m_l = jnp.full((tq, LANES), -jnp.inf, F32)
        for c in range(nck):
            s_c = lax.dot_general(q, kcat_ref[h, c * kc:(c + 1) * kc, :], nt, preferred_element_type=F32)
            s_ref[h % MLA_SCORE_SLOTS, :, c * kc:(c + 1) * kc] = s_c
            for b in range(kc // LANES):
                m_l = jnp.maximum(m_l, s_c[:, b * LANES:(b + 1) * LANES])
        m = jnp.max(m_l, axis=-1, keepdims=True)
        l_l = jnp.zeros((tq, LANES), F32)
        o = jnp.zeros((tq, V_HEAD), F32)
        for c in range(nck):
            p = jnp.exp2(s_ref[h % MLA_SCORE_SLOTS, :, c * kc:(c + 1) * kc] - m)
            for b in range(kc // LANES):
                l_l = l_l + p[:, b * LANES:(b + 1) * LANES]
            o = o + _dot(p.astype(BF16),
                         kv_ref[c * kc:(c + 1) * kc, h * Q_HEAD_PAD + LANES:(h + 1) * Q_HEAD_PAD])
        l = jnp.sum(l_l, axis=-1, keepdims=True)
        o_ref[:, h * V_HEAD:(h + 1) * V_HEAD] = (o / l).astype(BF16)


def _mla_attention(q, kv, kr, row0, nb, s):
    tq = _tile(s, 512)
    kc = _tile(s, MLA_KEY_CHUNK)
    nq = s // tq
    qb0 = row0 // tq
    sb0 = row0 // s
    hp = MLA_HEADS_PER_STEP
    return pl.pallas_call(
        functools.partial(_mla_kernel, kc=kc), grid=(nb, MLA_HEADS // hp, nq),
        in_specs=[pl.BlockSpec((tq, hp * Q_HEAD_PAD), lambda b, h, i: (qb0 + b * nq + i, h)),
                  pl.BlockSpec((s, hp * Q_HEAD_PAD), lambda b, h, i: (sb0 + b, h)),
                  pl.BlockSpec((s, LANES), lambda b, h, i: (sb0 + b, 0))],
        out_specs=pl.BlockSpec((tq, hp * V_HEAD), lambda b, h, i: (b * nq + i, h)),
        out_shape=jax.ShapeDtypeStruct((nb * s, MLA_HEADS * V_HEAD), BF16),
        scratch_shapes=[pltpu.VMEM((hp, s, Q_HEAD_PAD), BF16), pltpu.VMEM((MLA_SCORE_SLOTS, tq, s), F32)],
        compiler_params=_params("parallel", "parallel", "arbitrary"), name="mla_attn")(q, kv, kr)


def _sgu_kernel(u_ref, v_ref, g_ref, b_ref, ws_ref, bs_ref, o_ref, *, chunks):
    v = _layer_norm(v_ref[...].astype(F32), g_ref[...], b_ref[...]).astype(BF16)
    gw = SGU_WIDTH // SGU_GROUPS
    for c in range(chunks):
        r0 = c * SGU_CHUNK
        for g in range(SGU_GROUPS):
            sg = _dot(ws_ref[g], v[r0:r0 + SGU_CHUNK, g * gw:(g + 1) * gw])
            sg = sg + bs_ref[:, g * gw:(g + 1) * gw]
            u = u_ref[r0:r0 + SGU_CHUNK, g * gw:(g + 1) * gw].astype(F32)
            o_ref[r0:r0 + SGU_CHUNK, g * gw:(g + 1) * gw] = (u * sg).astype(BF16)


def _sgu(zs, ln_g, ln_b, ws, bs_full):
    m = zs.shape[0]
    tc = _tile(m, 2 * SGU_CHUNK)
    return pl.pallas_call(
        functools.partial(_sgu_kernel, chunks=tc // SGU_CHUNK), grid=(m // tc,),
        in_specs=[pl.BlockSpec((tc, SGU_WIDTH), lambda i: (i, 0)),
                  pl.BlockSpec((tc, SGU_WIDTH), lambda i: (i, 1)),
                  pl.BlockSpec((1, SGU_WIDTH), lambda i: (0, 0)),
                  pl.BlockSpec((1, SGU_WIDTH), lambda i: (0, 0)),
                  pl.BlockSpec((SGU_GROUPS, SGU_CHUNK, SGU_CHUNK), lambda i: (0, 0, 0)),
                  pl.BlockSpec((SGU_CHUNK, SGU_WIDTH), lambda i: (0, 0))],
        out_specs=pl.BlockSpec((tc, SGU_WIDTH), lambda i: (i, 0)),
        out_shape=jax.ShapeDtypeStruct((m, SGU_WIDTH), BF16),
        compiler_params=_params("parallel"), name="sgu")(zs, zs, ln_g, ln_b, ws, bs_full)


def _chan_dft_kernel(z_ref, w_ref, c_ref, s_ref):
    acc = _dot(z_ref[...], w_ref[...])
    c_ref[...] = acc[:, :FNET_GW].astype(BF16)
    s_ref[...] = acc[:, FNET_GW:].astype(BF16)


def _chan_dft(zf, w_cs):
    m = zf.shape[0]
    tm = _tile(m, 1024)
    blk = pl.BlockSpec((tm, FNET_GW), lambda i, g: (i, g))
    return pl.pallas_call(
        _chan_dft_kernel, grid=(m // tm, FNET_GROUPS),
        in_specs=[blk, pl.BlockSpec((FNET_GW, 2 * FNET_GW), lambda i, g: (0, 0))],
        out_specs=[blk, blk],
        out_shape=[jax.ShapeDtypeStruct((m, FNET_WIDTH), BF16)] * 2,
        compiler_params=_params("parallel", "arbitrary"), name="chan_dft")(zf, w_cs)


def _seq_dft_kernel(cs_ref, ss_ref, xc_ref, xs_ref, y_ref):
    y = _dot(cs_ref[...], xc_ref[...]) - _dot(ss_ref[...], xs_ref[...])
    y_ref[...] = y.astype(BF16)


def _seq_dft(cs, ss, xc, xs, row0, nb, s):
    tm = _tile(s, 512)
    tn = 512
    ni = s // tm
    sb0 = row0 // s
    return pl.pallas_call(
        _seq_dft_kernel, grid=(nb, FNET_WIDTH // tn, ni),
        in_specs=[pl.BlockSpec((tm, s), lambda b, j, i: (i, 0)),
                  pl.BlockSpec((tm, s), lambda b, j, i: (i, 0)),
                  pl.BlockSpec((s, tn), lambda b, j, i: (sb0 + b, j)),
                  pl.BlockSpec((s, tn), lambda b, j, i: (sb0 + b, j))],
        out_specs=pl.BlockSpec((tm, tn), lambda b, j, i: (b * ni + i, j)),
        out_shape=jax.ShapeDtypeStruct((nb * s, FNET_WIDTH), BF16),
        compiler_params=_params("parallel", "parallel", "arbitrary"), name="seq_dft")(cs, ss, xc, xs)


DFT_SPLIT = 64


def _dft_tables(n):
    k = lax.iota(jnp.int32, n)
    j1 = lax.iota(jnp.int32, n // DFT_SPLIT)
    j0 = lax.iota(jnp.int32, DFT_SPLIT)
    w = 2.0 * math.pi / n
    ang_a = ((j1[:, None] * k[None, :] * DFT_SPLIT) % n).astype(F32) * w
    ang_b = ((j0[:, None] * k[None, :]) % n).astype(F32) * w
    ca, sa = jnp.cos(ang_a)[:, None, :], jnp.sin(ang_a)[:, None, :]
    cb, sb = jnp.cos(ang_b)[None, :, :], jnp.sin(ang_b)[None, :, :]
    sc = n ** -0.5
    cos = ((ca * cb - sa * sb) * sc).reshape(n, n)
    sin = ((sa * cb + ca * sb) * sc).reshape(n, n)
    return cos.astype(BF16), sin.astype(BF16)


def _branch_kernel(a0, a1, a2, w0, w1, w2, g0, g1, g2, m_ref):
    m = (g0[...].astype(F32) * _dot(a0[...], w0[...])
         + g1[...].astype(F32) * _dot(a1[...], w1[...])
         + g2[...].astype(F32) * _dot(a2[...], w2[...]))
    m_ref[...] = m.astype(BF16)


def _branch_merge(a0, a1, a2, w0, w1, w2, gates):
    m = a0.shape[0]
    tm = _tile(m, 512)
    tn = 512
    nb = D_MODEL // tn
    a_spec = pl.BlockSpec((tm, D_MODEL), lambda i, j: (i, 0))
    w_spec = pl.BlockSpec((D_MODEL, tn), lambda i, j: (0, j))
    g_specs = [pl.BlockSpec((tm, tn), functools.partial(lambda i, j, r: (i, r * nb + j), r=r))
               for r in range(N_BRANCH)]
    return pl.pallas_call(
        _branch_kernel, grid=(m // tm, nb),
        in_specs=[a_spec] * 3 + [w_spec] * 3 + g_specs,
        out_specs=pl.BlockSpec((tm, tn), lambda i, j: (i, j)),
        out_shape=jax.ShapeDtypeStruct((m, D_MODEL), BF16),
        compiler_params=_params("parallel", "arbitrary"), name="branch_merge")(
            a0, a1, a2, w0, w1, w2, gates, gates, gates)


def _proj_ln_kernel(a_ref, w_ref, x_ref, g_ref, b_ref, xo_ref, xb_ref, *, alpha):
    y = alpha * x_ref[...] + _dot(a_ref[...], w_ref[...])
    xn = _layer_norm(y, g_ref[...], b_ref[...])
    xo_ref[...] = xn
    xb_ref[...] = xn.astype(BF16)


def _proj_ln(a, w, x, g, b, alpha):
    m = a.shape[0]
    tm = _tile(m, 512)
    row = pl.BlockSpec((tm, D_MODEL), lambda i: (i, 0))
    vec = pl.BlockSpec((1, D_MODEL), lambda i: (0, 0))
    return pl.pallas_call(
        functools.partial(_proj_ln_kernel, alpha=alpha), grid=(m // tm,),
        in_specs=[row, pl.BlockSpec((D_MODEL, D_MODEL), lambda i: (0, 0)), row, vec, vec],
        out_specs=[row, row],
        out_shape=[jax.ShapeDtypeStruct((m, D_MODEL), F32), jax.ShapeDtypeStruct((m, D_MODEL), BF16)],
        compiler_params=_params("parallel"), name="proj_ln")(a, w, x, g, b)


def _xattn_kernel(q_ref, k_ref, v_ref, o_ref):
    for h in range(X_HEADS):
        sl = slice(h * X_HEAD_DIM, (h + 1) * X_HEAD_DIM)
        s = lax.dot_general(q_ref[:, sl], k_ref[:, sl], (((1,), (1,)), ((), ())),
                            preferred_element_type=F32)
        p = jnp.exp(s - jnp.max(s, axis=-1, keepdims=True))
        l = jnp.sum(p, axis=-1, keepdims=True)
        o_ref[:, sl] = (_dot(p.astype(BF16), v_ref[:, sl]) / l).astype(BF16)


def _xattn(q, km, vm, mem_len, groups):
    m = q.shape[0]
    tq = _tile(min(s for _, s in groups), 512)
    bounds = []
    t0 = r0 = 0
    for rows, s in groups:
        bounds.append((t0, s // tq, r0))
        t0 += rows // tq
        r0 += rows // s

    def mem_block(i):
        blk = jnp.int32(0)
        for first, per, req0 in bounds:
            blk = jnp.where(i >= first, req0 + (i - first) // per, blk)
        return blk

    row = pl.BlockSpec((tq, D_MODEL), lambda i: (i, 0))
    mem = pl.BlockSpec((mem_len, D_MODEL), lambda i: (mem_block(i), 0))
    return pl.pallas_call(
        _xattn_kernel, grid=(m // tq,), in_specs=[row, mem, mem], out_specs=row,
        out_shape=jax.ShapeDtypeStruct((m, D_MODEL), BF16),
        compiler_params=_params("parallel"), name="xattn")(q, km, vm)


def _router_kernel(x_ref, w_ref, b_ref, ir_ref, p_ref, cnt_ref, carry_ref):
    i = pl.program_id(0)

    @pl.when(i == 0)
    def _():
        carry_ref[...] = jnp.zeros_like(carry_ref)

    tm = x_ref.shape[0]
    logits = jnp.dot(x_ref[...], w_ref[...], preferred_element_type=F32,
                     precision=lax.Precision.HIGHEST) + b_ref[...]
    lane = lax.broadcasted_iota(jnp.int32, (tm, LANES), 1)
    tops, idxs = [], []
    cur = logits
    for _ in range(TOP_K):
        mx = jnp.max(cur, axis=-1, keepdims=True)
        ix = jnp.min(jnp.where(cur == mx, lane, LANES), axis=-1, keepdims=True)
        tops.append(mx)
        idxs.append(ix)
        cur = jnp.where(lane == ix, NEG_BIG * 2.0, cur)
    exps = [jnp.exp(t - tops[0]) for t in tops]
    denom = exps[0] + exps[1] + exps[2] + exps[3]

    onehot = jnp.zeros((tm, LANES), F32)
    for ix in idxs:
        onehot = onehot + (lane == ix).astype(F32)
    r = lax.broadcasted_iota(jnp.int32, (tm, tm), 0)
    c = lax.broadcasted_iota(jnp.int32, (tm, tm), 1)
    lower = (c < r).astype(BF16)
    before = _dot(lower, onehot.astype(BF16)) + carry_ref[...]

    ir = jnp.zeros((tm, LANES), jnp.int32)
    pr = jnp.zeros((tm, LANES), F32)
    for k in range(TOP_K):
        rank = jnp.sum(jnp.where(lane == idxs[k], before, 0.0), axis=-1, keepdims=True)
        ir = jnp.where(lane == k, idxs[k], ir)
        ir = jnp.where(lane == TOP_K + k, rank.astype(jnp.int32), ir)
        pr = jnp.where(lane == k, exps[k] / denom, pr)
    ir_ref[...] = ir
    p_ref[...] = pr
    carry_ref[...] = carry_ref[...] + jnp.sum(onehot, axis=0, keepdims=True)
    cnt_ref[...] = carry_ref[...]


def _router(x, w_pad, b_pad):
    m = x.shape[0]
    tm = _tile(m, 512)
    return pl.pallas_call(
        _router_kernel, grid=(m // tm,),
        in_specs=[pl.BlockSpec((tm, D_MODEL), lambda i: (i, 0)),
                  pl.BlockSpec((D_MODEL, LANES), lambda i: (0, 0)),
                  pl.BlockSpec((1, LANES), lambda i: (0, 0))],
        out_specs=[pl.BlockSpec((tm, LANES), lambda i: (i, 0)),
                   pl.BlockSpec((tm, LANES), lambda i: (i, 0)),
                   pl.BlockSpec((1, LANES), lambda i: (0, 0))],
        out_shape=[jax.ShapeDtypeStruct((m, LANES), jnp.int32),
                   jax.ShapeDtypeStruct((m, LANES), F32),
                   jax.ShapeDtypeStruct((1, LANES), F32)],
        scratch_shapes=[pltpu.VMEM((1, LANES), F32)],
        compiler_params=_params("arbitrary"), name="router")(x, w_pad, b_pad)


def _moe_kernel(te_ref, nu_ref, idx_cur, idx_nxt, x_hbm, wg_ref, wu_ref, bg_ref, bu_ref, wd_ref, bd_ref,
                out_ref, xbuf, xb_ref, sem, *, tm, nf, sub):
    j = pl.program_id(0)
    f = pl.program_id(1)
    n_used = nu_ref[0]

    def start_gather(idx_ref, slot):
        def body(g, carry):
            for u in range(SUBLANES):
                t = idx_ref[0, 0, g * SUBLANES + u]
                pltpu.make_async_copy(
                    x_hbm.at[lax.shift_right_logical(t, 3), pl.ds(lax.bitwise_and(t, SUBLANES - 1), 1), :],
                    xbuf.at[slot, g, pl.ds(u, 1), :], sem.at[slot]).start()
            return carry
        lax.fori_loop(0, tm // SUBLANES, body, 0)

    @pl.when(jnp.logical_and(j == 0, f == 0))
    def _():
        start_gather(idx_cur, 0)

    @pl.when(jnp.logical_and(f == 0, j + 1 < n_used))
    def _():
        start_gather(idx_nxt, (j + 1) % 2)

    @pl.when(j < n_used)
    def _():
        slot = j % 2

        @pl.when(f == 0)
        def _():
            pltpu.make_async_copy(x_hbm.at[pl.ds(0, tm // SUBLANES)], xbuf.at[slot], sem.at[slot]).wait()
            xb_ref[...] = xbuf[slot].reshape(tm, D_MODEL).astype(BF16)

        x = xb_ref[...]
        part = None
        for c in range(wg_ref.shape[3] // sub):
            sl = slice(c * sub, (c + 1) * sub)
            gate = jnp.minimum(_dot(x, wg_ref[0, 0, :, sl]) + bg_ref[0, 0, :, sl], SWIGLU_LIMIT)
            up = jnp.clip(_dot(x, wu_ref[0, 0, :, sl]) + bu_ref[0, 0, :, sl], -SWIGLU_LIMIT, SWIGLU_LIMIT)
            act = (up + 1.0) * gate * jax.nn.sigmoid(SWIGLU_ALPHA * gate)
            d = _dot(act.astype(BF16), wd_ref[0, 0, sl, :])
            part = d if part is None else part + d

        @pl.when(f == 0)
        def _():
            out_ref[...] = part + bd_ref[0, 0]

        @pl.when(f > 0)
        def _():
            out_ref[...] += part

    @pl.when(jnp.logical_and(j >= n_used, f == nf - 1))
    def _():
        out_ref[...] = jnp.zeros_like(out_ref)


MOE_FF_BLOCK = 1024
MOE_FF_SUB = 512


def _moe_experts(layer, tile_expert, n_used, src_tok, x, w_gu, b_gu, w_down, b_down, tm):
    n_tiles = tile_expert.shape[0]
    tf = MOE_FF_BLOCK
    nf = D_FF // tf
    idx3 = src_tok.reshape(n_tiles, 1, tm)

    def fchunk(j, f, nu):
        return jnp.where(j < nu[0], f, nf - 1)

    grid_spec = pltpu.PrefetchScalarGridSpec(
        num_scalar_prefetch=2, grid=(n_tiles, nf),
        in_specs=[
            pl.BlockSpec((1, 1, tm), lambda j, f, te, nu: (j, 0, 0), memory_space=pltpu.SMEM),
            pl.BlockSpec((1, 1, tm), lambda j, f, te, nu: (jnp.minimum(j + 1, n_tiles - 1), 0, 0),
                         memory_space=pltpu.SMEM),
            pl.BlockSpec(memory_space=pl.ANY),
            pl.BlockSpec((1, 1, D_MODEL, tf), lambda j, f, te, nu: (layer, te[j], 0, fchunk(j, f, nu))),
            pl.BlockSpec((1, 1, D_MODEL, tf), lambda j, f, te, nu: (layer, te[j], 0, nf + fchunk(j, f, nu))),
            pl.BlockSpec((1, 1, 1, tf), lambda j, f, te, nu: (layer, te[j], 0, fchunk(j, f, nu))),
            pl.BlockSpec((1, 1, 1, tf), lambda j, f, te, nu: (layer, te[j], 0, nf + fchunk(j, f, nu))),
            pl.BlockSpec((1, 1, tf, D_MODEL), lambda j, f, te, nu: (layer, te[j], fchunk(j, f, nu), 0)),
            pl.BlockSpec((1, 1, 1, D_MODEL), lambda j, f, te, nu: (layer, te[j], 0, 0)),
        ],
        out_specs=pl.BlockSpec((tm, D_MODEL), lambda j, f, te, nu: (j, 0)),
        scratch_shapes=[pltpu.VMEM((2, tm // SUBLANES, SUBLANES, D_MODEL), F32),
                        pltpu.VMEM((tm, D_MODEL), BF16),
                        pltpu.SemaphoreType.DMA((2,))])
    return pl.pallas_call(
        functools.partial(_moe_kernel, tm=tm, nf=nf, sub=MOE_FF_SUB), grid_spec=grid_spec,
        out_shape=jax.ShapeDtypeStruct((n_tiles * tm, D_MODEL), F32),
        compiler_params=_params("arbitrary", "arbitrary"), name="moe_experts")(
            tile_expert, n_used, idx3, idx3, x.reshape(-1, SUBLANES, D_MODEL),
            w_gu, w_gu, b_gu, b_gu, w_down, b_down)


def _combine_kernel(pos_cur, pos_nxt, ys_hbm, p_ref, x_ref, g_ref, b_ref, xo_ref, xb_ref, buf, sem,
                    *, tm, alpha):
    i = pl.program_id(0)
    n = pl.num_programs(0)

    def start_gather(pos_ref, slot):
        def body(g, carry):
            for u in range(SUBLANES):
                for k in range(TOP_K):
                    t = pos_ref[0, 0, (g * SUBLANES + u) * TOP_K + k]
                    pltpu.make_async_copy(
                        ys_hbm.at[lax.shift_right_logical(t, 3), pl.ds(lax.bitwise_and(t, SUBLANES - 1), 1), :],
                        buf.at[slot, k, g, pl.ds(u, 1), :], sem.at[slot]).start()
            return carry
        lax.fori_loop(0, tm // SUBLANES, body, 0)

    @pl.when(i == 0)
    def _():
        start_gather(pos_cur, 0)

    @pl.when(i + 1 < n)
    def _():
        start_gather(pos_nxt, (i + 1) % 2)

    slot = i % 2
    for k in range(TOP_K):
        pltpu.make_async_copy(ys_hbm.at[pl.ds(0, tm // SUBLANES)], buf.at[slot, k], sem.at[slot]).wait()
    p = p_ref[...]
    y = alpha * x_ref[...]
    for k in range(TOP_K):
        y = y + p[:, k:k + 1] * buf[slot, k].reshape(tm, D_MODEL)
    xn = _layer_norm(y, g_ref[...], b_ref[...])
    xo_ref[...] = xn
    xb_ref[...] = xn.astype(BF16)


def _combine_ln(pos, ys, probs, x, g, b, alpha):
    m = x.shape[0]
    tm = _tile(m, 256)
    nt = m // tm
    pos3 = pos.reshape(nt, 1, tm * TOP_K)
    row = pl.BlockSpec((tm, D_MODEL), lambda i: (i, 0))
    vec = pl.BlockSpec((1, D_MODEL), lambda i: (0, 0))
    return pl.pallas_call(
        functools.partial(_combine_kernel, tm=tm, alpha=alpha), grid=(nt,),
        in_specs=[pl.BlockSpec((1, 1, tm * TOP_K), lambda i: (i, 0, 0), memory_space=pltpu.SMEM),
                  pl.BlockSpec((1, 1, tm * TOP_K), lambda i: (jnp.minimum(i + 1, nt - 1), 0, 0),
                               memory_space=pltpu.SMEM),
                  pl.BlockSpec(memory_space=pl.ANY),
                  pl.BlockSpec((tm, LANES), lambda i: (i, 0)),
                  row, vec, vec],
        out_specs=[row, row],
        out_shape=[jax.ShapeDtypeStruct((m, D_MODEL), F32), jax.ShapeDtypeStruct((m, D_MODEL), BF16)],
        scratch_shapes=[pltpu.VMEM((2, TOP_K, tm // SUBLANES, SUBLANES, D_MODEL), F32),
                        pltpu.SemaphoreType.DMA((2,))],
        compiler_params=_params("arbitrary"), name="combine_ln")(
            pos3, pos3, ys.reshape(-1, SUBLANES, D_MODEL), probs, x, g, b)


def _routing_plan(ir, counts, tm):
    m = ir.shape[0]
    n_tiles = (m * TOP_K) // tm + N_EXPERTS
    idx = ir[:, :TOP_K]
    rank = ir[:, TOP_K:2 * TOP_K]
    cnt = counts[0, :N_EXPERTS].astype(jnp.int32)
    tiles_per = (cnt + tm - 1) // tm
    tile_end = jnp.cumsum(tiles_per)
    tile_start = tile_end - tiles_per
    n_used = tile_end[-1]
    pos = tile_start[idx] * tm + rank
    tile_ids = jnp.arange(n_tiles, dtype=jnp.int32)
    tile_expert = jnp.searchsorted(tile_end, jnp.minimum(tile_ids, n_used - 1), side="right")
    tile_expert = jnp.minimum(tile_expert, N_EXPERTS - 1).astype(jnp.int32)
    tok = jnp.broadcast_to(jnp.arange(m, dtype=jnp.int32)[:, None], (m, TOP_K))
    src_tok = jnp.zeros((n_tiles * tm,), jnp.int32).at[pos.reshape(-1)].set(tok.reshape(-1))
    return pos.astype(jnp.int32), tile_expert, n_used.reshape(1).astype(jnp.int32), src_tok


def _rope_table(groups):
    inv = 1.0 / (ROPE_THETA ** (jnp.arange(0, QK_ROPE, 2, dtype=F32) / QK_ROPE))
    pos = jnp.concatenate([jnp.tile(jnp.arange(s, dtype=F32), rows // s) for rows, s in groups])
    ang = pos[:, None] * inv[None, :]
    cos, sin = jnp.cos(ang), jnp.sin(ang)
    z = jnp.zeros_like(cos)
    return jnp.concatenate([cos, cos, z, z, z, sin, z, z, -sin, z, z, z], axis=1)


def kernel(x_prompt, x_sample, mem_prompt, mem_sample, w_in, b_gate, mla_q_norm, mla_kv_norm, w_uq, w_ukv,
           w_mla_o, sgu_ln_g, sgu_ln_b, sgu_ws, sgu_bs, w_sgu_o, w_fnet_o, w_out, ln1_g, ln1_b, w_cq, w_ck,
           w_cv, w_co, ln2_g, ln2_b, w_router, b_router, w_gu, b_gu, w_down, b_down, ln3_g, ln3_b):
    depth = w_in.shape[0]
    alpha = (2 * depth) ** 0.25
    bp, sp, _ = x_prompt.shape
    bs_, ss, _ = x_sample.shape
    mem_len = mem_prompt.shape[1]
    groups = ((bp * sp, sp), (bs_ * ss, ss))
    assert (bp * sp) % ss == 0 and sp % SGU_CHUNK == 0 and ss % SGU_CHUNK == 0
    m = bp * sp + bs_ * ss

    x = jnp.concatenate([x_prompt.reshape(bp * sp, D_MODEL), x_sample.reshape(bs_ * ss, D_MODEL)], axis=0)
    xb = x.astype(BF16)
    memb = jnp.concatenate([mem_prompt.reshape(bp * mem_len, D_MODEL),
                            mem_sample.reshape(bs_ * mem_len, D_MODEL)], axis=0).astype(BF16)

    tab = _rope_table(groups)
    dft_c, dft_s = _dft_tables(FNET_GW)
    w_chan = jnp.concatenate([dft_c, dft_s], axis=1)
    seq_tabs = {s: _dft_tables(s) for s in {sp, ss}}
    moe_tm = _tile(m * TOP_K, 512)
    w_gu_b = w_gu.astype(BF16)
    w_down_b = w_down.astype(BF16)
    b_gu4 = b_gu.reshape(depth, N_EXPERTS, 1, 2 * D_FF)
    b_down4 = b_down.reshape(depth, N_EXPERTS, 1, D_MODEL)

    for l in range(depth):
        w_lat = jnp.pad(w_in[l, :, :OFF_SGU], ((0, 0), (0, LAT_PAD - OFF_SGU))).astype(BF16)
        w_sgu_in = w_in[l, :, OFF_SGU:OFF_FNET].astype(BF16)
        w_fnet_in = w_in[l, :, OFF_FNET:OFF_GATE].astype(BF16)
        w_gate_in = w_in[l, :, OFF_GATE:].astype(BF16)
        wq = jnp.pad(w_uq[l].reshape(Q_LORA, MLA_HEADS, QK_NOPE + QK_ROPE),
                     ((0, 0), (0, 0), (0, Q_HEAD_PAD - QK_NOPE - QK_ROPE))
                     ).reshape(Q_LORA, MLA_HEADS * Q_HEAD_PAD).astype(BF16)
        bs_full = jnp.repeat(sgu_bs[l].T, SGU_WIDTH // SGU_GROUPS, axis=1)

        cq, ckv, kr = _latent(xb, w_lat, mla_q_norm[l][None], mla_kv_norm[l][None], tab)
        q = _qproj(cq, wq, tab)
        (kv,) = _matmul(ckv, w_ukv[l].astype(BF16), _ep_identity, [BF16], name="kvproj")
        o_parts, row0 = [], 0
        for rows, s in groups:
            o_parts.append(_mla_attention(q, kv, kr, row0, rows // s, s))
            row0 += rows
        o_mla = jnp.concatenate(o_parts, axis=0)

        (zs,) = _matmul(xb, w_sgu_in, _ep_gelu, [BF16], name="sgu_in")
        us = _sgu(zs, sgu_ln_g[l][None], sgu_ln_b[l][None], sgu_ws[l].astype(BF16), bs_full)

        (zf,) = _matmul(xb, w_fnet_in, _ep_identity, [BF16], name="fnet_in")
        xc, xs = _chan_dft(zf, w_chan)
        f_parts, row0 = [], 0
        for rows, s in groups:
            cs, sn = seq_tabs[s]
            f_parts.append(_seq_dft(cs, sn, xc, xs, row0, rows // s, s))
            row0 += rows
        fy = jnp.concatenate(f_parts, axis=0)

        (gates,) = _matmul(xb, w_gate_in, _ep_sigmoid_bias, [BF16], extras=[("col", b_gate[l][None])],
                           name="gates")
        mrg = _branch_merge(o_mla, us, fy, w_mla_o[l].astype(BF16), w_sgu_o[l].astype(BF16),
                            w_fnet_o[l].astype(BF16), gates)
        x, xb = _proj_ln(mrg, w_out[l].astype(BF16), x, ln1_g[l][None], ln1_b[l][None], alpha)

        (qx,) = _matmul(xb, w_cq[l].astype(BF16), functools.partial(_ep_scale, scale=X_HEAD_DIM ** -0.5),
                        [BF16], name="xq")
        (km,) = _matmul(memb, w_ck[l].astype(BF16), _ep_identity, [BF16], name="xk")
        (vm,) = _matmul(memb, w_cv[l].astype(BF16), _ep_identity, [BF16], name="xv")
        ox = _xattn(qx, km, vm, mem_len, groups)
        x, xb = _proj_ln(ox, w_co[l].astype(BF16), x, ln2_g[l][None], ln2_b[l][None], alpha)

        w_r = jnp.pad(w_router[l], ((0, 0), (0, LANES - N_EXPERTS)))
        b_r = jnp.pad(b_router[l], (0, LANES - N_EXPERTS), constant_values=NEG_BIG)[None]
        ir, probs, counts = _router(x, w_r, b_r)
        pos, tile_expert, n_used, src_tok = _routing_plan(ir, counts, moe_tm)
        ys = _moe_experts(l, tile_expert, n_used, src_tok, x, w_gu_b, b_gu4, w_down_b, b_down4, moe_tm)
        x, xb = _combine_ln(pos, ys, probs, x, ln3_g[l][None], ln3_b[l][None], alpha)

    y_prompt = x[:bp * sp].reshape(bp, sp, D_MODEL)
    y_sample = x[bp * sp:].reshape(bs_, ss, D_MODEL)
    return (y_prompt, y_sample)
```

```python
import functools
import math

import jax
import jax.numpy as jnp
from jax import lax
from jax.experimental import pallas as pl
from jax.experimental.pallas import tpu as pltpu

D_MODEL = 2048
MLA_HEADS = 16
Q_LORA = 512
KV_LORA = 512
QK_NOPE = 128
QK_ROPE = 64
V_HEAD = 128
ROPE_THETA = 10000.0
MLA_SCALE = (QK_NOPE + QK_ROPE) ** -0.5
Q_PRESCALE = MLA_SCALE * math.log2(math.e)
SGU_CHUNK = 128
SGU_GROUPS = 4
SGU_WIDTH = D_MODEL
FNET_GROUPS = 4
FNET_WIDTH = D_MODEL
FNET_GW = FNET_WIDTH // FNET_GROUPS
X_HEADS = 4
X_HEAD_DIM = D_MODEL // X_HEADS
N_EXPERTS = 32
TOP_K = 4
D_FF = D_MODEL
SWIGLU_LIMIT = 7.0
SWIGLU_ALPHA = 1.702
N_BRANCH = 3
LN_EPS = 1e-5
RMS_EPS = 1e-6
OFF_KV = Q_LORA
OFF_KR = OFF_KV + KV_LORA
OFF_SGU = OFF_KR + QK_ROPE
OFF_FNET = OFF_SGU + 2 * SGU_WIDTH
OFF_GATE = OFF_FNET + FNET_WIDTH

LANES = 128
SUBLANES = 8
Q_HEAD_PAD = 2 * LANES
LAT_PAD = OFF_SGU + (LANES - QK_ROPE)
VMEM_LIMIT = 56 * 1024 * 1024
NEG_BIG = -1e30
GATHER_DMA_PRIORITY = 1

BF16 = jnp.bfloat16
F32 = jnp.float32


def _tile(n, pref):
    if n <= pref:
        return n
    t = pref
    while n % t:
        t //= 2
    return t


def _params(*sem):
    return pltpu.CompilerParams(dimension_semantics=sem, vmem_limit_bytes=VMEM_LIMIT)


def _dot(a, b):
    return jnp.dot(a, b, preferred_element_type=F32)


def _layer_norm(x, g, b):
    mu = jnp.mean(x, axis=-1, keepdims=True)
    xc = x - mu
    var = jnp.mean(xc * xc, axis=-1, keepdims=True)
    return xc * lax.rsqrt(var + LN_EPS) * g + b


def _rope_lanes(z, tab):
    return (z * tab[:, :LANES]
            + pltpu.roll(z, QK_ROPE // 2, axis=1) * tab[:, LANES:2 * LANES]
            + pltpu.roll(z, LANES - QK_ROPE // 2, axis=1) * tab[:, 2 * LANES:])


def _mm_kernel(*refs, n_extra, n_out, epilogue):
    a_ref, w_ref = refs[0], refs[1]
    extra = refs[2:2 + n_extra]
    outs = refs[2 + n_extra:2 + n_extra + n_out]
    acc = _dot(a_ref[...], w_ref[...])
    res = epilogue(acc, *[e[...] for e in extra])
    for o, r in zip(outs, res):
        o[...] = r.astype(o.dtype)


def _matmul(a, w, epilogue, out_dtypes, extras=(), tm=1024, tn=1024, name="mm"):
    m, k = a.shape
    n = w.shape[1]
    tm = _tile(m, tm)
    tn = _tile(n, tn)
    in_specs = [pl.BlockSpec((tm, k), lambda i, j: (i, 0)),
                pl.BlockSpec((k, tn), lambda i, j: (0, j))]
    args = [a, w]
    for kind, arr in extras:
        if kind == "col":
            in_specs.append(pl.BlockSpec((1, tn), lambda i, j: (0, j)))
        else:
            in_specs.append(pl.BlockSpec((tm, tn), lambda i, j: (i, j)))
        args.append(arr)
    out_shape = [jax.ShapeDtypeStruct((m, n), dt) for dt in out_dtypes]
    out_specs = [pl.BlockSpec((tm, tn), lambda i, j: (i, j)) for _ in out_dtypes]
    kern = functools.partial(_mm_kernel, n_extra=len(extras), n_out=len(out_dtypes), epilogue=epilogue)
    return pl.pallas_call(
        kern, grid=(m // tm, n // tn), in_specs=in_specs, out_specs=out_specs, out_shape=out_shape,
        compiler_params=_params("parallel", "arbitrary"), name=name)(*args)


def _ep_identity(acc):
    return (acc,)


def _ep_gelu(acc):
    return (0.5 * acc * (1.0 + lax.erf(acc * (2.0 ** -0.5))),)


def _ep_sigmoid_bias(acc, b):
    return (jax.nn.sigmoid(acc + b),)


def _ep_scale(acc, *, scale):
    return (acc * scale,)


def _latent_kernel(x_ref, w_ref, qg_ref, kvg_ref, tab_ref, cq_ref, ckv_ref, kr_ref):
    z = _dot(x_ref[...], w_ref[...])

    def rms(t, g):
        return t * lax.rsqrt(jnp.mean(t * t, axis=-1, keepdims=True) + RMS_EPS) * g

    cq_ref[...] = rms(z[:, :OFF_KV], qg_ref[...]).astype(BF16)
    ckv_ref[...] = rms(z[:, OFF_KV:OFF_KR], kvg_ref[...]).astype(BF16)
    kr_ref[...] = _rope_lanes(z[:, OFF_KR:], tab_ref[...]).astype(BF16)


def _latent(xb, w_lat, qg, kvg, tab):
    m, k = xb.shape
    tm = _tile(m, 512)
    return pl.pallas_call(
        _latent_kernel, grid=(m // tm,),
        in_specs=[pl.BlockSpec((tm, k), lambda i: (i, 0)),
                  pl.BlockSpec((k, LAT_PAD), lambda i: (0, 0)),
                  pl.BlockSpec((1, Q_LORA), lambda i: (0, 0)),
                  pl.BlockSpec((1, KV_LORA), lambda i: (0, 0)),
                  pl.BlockSpec((tm, 3 * LANES), lambda i: (i, 0))],
        out_specs=[pl.BlockSpec((tm, Q_LORA), lambda i: (i, 0)),
                   pl.BlockSpec((tm, KV_LORA), lambda i: (i, 0)),
                   pl.BlockSpec((tm, LANES), lambda i: (i, 0))],
        out_shape=[jax.ShapeDtypeStruct((m, Q_LORA), BF16),
                   jax.ShapeDtypeStruct((m, KV_LORA), BF16),
                   jax.ShapeDtypeStruct((m, LANES), BF16)],
        compiler_params=_params("parallel"), name="latent")(xb, w_lat, qg, kvg, tab)


def _qproj_kernel(c_ref, w_ref, tab_ref, q_ref, *, heads_per_tile):
    acc = _dot(c_ref[...], w_ref[...])
    tab = tab_ref[...]
    for h in range(heads_per_tile):
        lo = h * Q_HEAD_PAD
        q_ref[:, lo:lo + LANES] = (acc[:, lo:lo + LANES] * Q_PRESCALE).astype(BF16)
        q_ref[:, lo + LANES:lo + Q_HEAD_PAD] = (
            _rope_lanes(acc[:, lo + LANES:lo + Q_HEAD_PAD], tab) * Q_PRESCALE).astype(BF16)


def _qproj(cq, wq, tab):
    m, k = cq.shape
    n = wq.shape[1]
    tm = _tile(m, 1024)
    tn = 4 * Q_HEAD_PAD
    return pl.pallas_call(
        functools.partial(_qproj_kernel, heads_per_tile=tn // Q_HEAD_PAD),
        grid=(m // tm, n // tn),
        in_specs=[pl.BlockSpec((tm, k), lambda i, j: (i, 0)),
                  pl.BlockSpec((k, tn), lambda i, j: (0, j)),
                  pl.BlockSpec((tm, 3 * LANES), lambda i, j: (i, 0))],
        out_specs=pl.BlockSpec((tm, tn), lambda i, j: (i, j)),
        out_shape=jax.ShapeDtypeStruct((m, n), BF16),
        compiler_params=_params("parallel", "arbitrary"), name="qproj")(cq, wq, tab)


MLA_HEADS_PER_STEP = 4
MLA_SCORE_SLOTS = 2
MLA_KEY_CHUNK = 512


def _mla_kernel(q_ref, kv_ref, kr_ref, o_ref, kcat_ref, s_ref, *, kc):
    seq = kv_ref.shape[0]
    tq = q_ref.shape[0]
    nck = seq // kc
    nt = (((1,), (1,)), ((), ()))

    @pl.when(pl.program_id(2) == 0)
    def _():
        for h in range(MLA_HEADS_PER_STEP):
            kcat_ref[h, :, :LANES] = kv_ref[:, h * Q_HEAD_PAD:h * Q_HEAD_PAD + LANES]
            kcat_ref[h, :, LANES:] = kr_ref[...]

    for h in range(MLA_HEADS_PER_STEP):
        q = q_ref[:, h * Q_HEAD_PAD:(h + 1) * Q_HEAD_PAD]
        m_l = jnp.full((tq, LANES), -jnp.inf, F32)
        for c in range(nck):
            s_c = lax.dot_general(q, kcat_ref[h, c * kc:(c + 1) * kc, :], nt, preferred_element_type=F32)
            s_ref[h % MLA_SCORE_SLOTS, :, c * kc:(c + 1) * kc] = s_c
            for b in range(kc // LANES):
                m_l = jnp.maximum(m_l, s_c[:, b * LANES:(b + 1) * LANES])
        m = jnp.max(m_l, axis=-1, keepdims=True)
        l_l = jnp.zeros((tq, LANES), F32)
        o = jnp.zeros((tq, V_HEAD), F32)
        for c in range(nck):
            p = jnp.exp2(s_ref[h % MLA_SCORE_SLOTS, :, c * kc:(c + 1) * kc] - m)
            for b in range(kc // LANES):
                l_l = l_l + p[:, b * LANES:(b + 1) * LANES]
            o = o + _dot(p.astype(BF16),
                         kv_ref[c * kc:(c + 1) * kc, h * Q_HEAD_PAD + LANES:(h + 1) * Q_HEAD_PAD])
        l = jnp.sum(l_l, axis=-1, keepdims=True)
        o_ref[:, h * V_HEAD:(h + 1) * V_HEAD] = (o / l).astype(BF16)


def _mla_attention(q, kv, kr, row0, nb, s):
    tq = _tile(s, 512)
    kc = _tile(s, MLA_KEY_CHUNK)
    nq = s // tq
    qb0 = row0 // tq
    sb0 = row0 // s
    hp = MLA_HEADS_PER_STEP
    return pl.pallas_call(
        functools.partial(_mla_kernel, kc=kc), grid=(nb, MLA_HEADS // hp, nq),
        in_specs=[pl.BlockSpec((tq, hp * Q_HEAD_PAD), lambda b, h, i: (qb0 + b * nq + i, h)),
                  pl.BlockSpec((s, hp * Q_HEAD_PAD), lambda b, h, i: (sb0 + b, h)),
                  pl.BlockSpec((s, LANES), lambda b, h, i: (sb0 + b, 0))],
        out_specs=pl.BlockSpec((tq, hp * V_HEAD), lambda b, h, i: (b * nq + i, h)),
        out_shape=jax.ShapeDtypeStruct((nb * s, MLA_HEADS * V_HEAD), BF16),
        scratch_shapes=[pltpu.VMEM((hp, s, Q_HEAD_PAD), BF16), pltpu.VMEM((MLA_SCORE_SLOTS, tq, s), F32)],
        compiler_params=_params("parallel", "parallel", "arbitrary"), name="mla_attn")(q, kv, kr)


def _sgu_kernel(u_ref, v_ref, g_ref, b_ref, ws_ref, bs_ref, o_ref, *, chunks):
    v = _layer_norm(v_ref[...].astype(F32), g_ref[...], b_ref[...]).astype(BF16)
    gw = SGU_WIDTH // SGU_GROUPS
    for c in range(chunks):
        r0 = c * SGU_CHUNK
        for g in range(SGU_GROUPS):
            sg = _dot(ws_ref[g], v[r0:r0 + SGU_CHUNK, g * gw:(g + 1) * gw])
            sg = sg + bs_ref[:, g * gw:(g + 1) * gw]
            u = u_ref[r0:r0 + SGU_CHUNK, g * gw:(g + 1) * gw].astype(F32)
            o_ref[r0:r0 + SGU_CHUNK, g * gw:(g + 1) * gw] = (u * sg).astype(BF16)


def _sgu(zs, ln_g, ln_b, ws, bs_full):
    m = zs.shape[0]
    tc = _tile(m, 2 * SGU_CHUNK)
    return pl.pallas_call(
        functools.partial(_sgu_kernel, chunks=tc // SGU_CHUNK), grid=(m // tc,),
        in_specs=[pl.BlockSpec((tc, SGU_WIDTH), lambda i: (i, 0)),
                  pl.BlockSpec((tc, SGU_WIDTH), lambda i: (i, 1)),
                  pl.BlockSpec((1, SGU_WIDTH), lambda i: (0, 0)),
                  pl.BlockSpec((1, SGU_WIDTH), lambda i: (0, 0)),
                  pl.BlockSpec((SGU_GROUPS, SGU_CHUNK, SGU_CHUNK), lambda i: (0, 0, 0)),
                  pl.BlockSpec((SGU_CHUNK, SGU_WIDTH), lambda i: (0, 0))],
        out_specs=pl.BlockSpec((tc, SGU_WIDTH), lambda i: (i, 0)),
        out_shape=jax.ShapeDtypeStruct((m, SGU_WIDTH), BF16),
        compiler_params=_params("parallel"), name="sgu")(zs, zs, ln_g, ln_b, ws, bs_full)


def _chan_dft_kernel(z_ref, w_ref, c_ref, s_ref):
    acc = _dot(z_ref[...], w_ref[...])
    c_ref[...] = acc[:, :FNET_GW].astype(BF16)
    s_ref[...] = acc[:, FNET_GW:].astype(BF16)


def _chan_dft(zf, w_cs):
    m = zf.shape[0]
    tm = _tile(m, 1024)
    blk = pl.BlockSpec((tm, FNET_GW), lambda i, g: (i, g))
    return pl.pallas_call(
        _chan_dft_kernel, grid=(m // tm, FNET_GROUPS),
        in_specs=[blk, pl.BlockSpec((FNET_GW, 2 * FNET_GW), lambda i, g: (0, 0))],
        out_specs=[blk, blk],
        out_shape=[jax.ShapeDtypeStruct((m, FNET_WIDTH), BF16)] * 2,
        compiler_params=_params("parallel", "arbitrary"), name="chan_dft")(zf, w_cs)


def _seq_dft_kernel(cs_ref, ss_ref, xc_ref, xs_ref, y_ref):
    y = _dot(cs_ref[...], xc_ref[...]) - _dot(ss_ref[...], xs_ref[...])
    y_ref[...] = y.astype(BF16)


def _seq_dft(cs, ss, xc, xs, row0, nb, s):
    tm = _tile(s, 512)
    tn = 512
    ni = s // tm
    sb0 = row0 // s
    return pl.pallas_call(
        _seq_dft_kernel, grid=(nb, FNET_WIDTH // tn, ni),
        in_specs=[pl.BlockSpec((tm, s), lambda b, j, i: (i, 0)),
                  pl.BlockSpec((tm, s), lambda b, j, i: (i, 0)),
                  pl.BlockSpec((s, tn), lambda b, j, i: (sb0 + b, j)),
                  pl.BlockSpec((s, tn), lambda b, j, i: (sb0 + b, j))],
        out_specs=pl.BlockSpec((tm, tn), lambda b, j, i: (b * ni + i, j)),
        out_shape=jax.ShapeDtypeStruct((nb * s, FNET_WIDTH), BF16),
        compiler_params=_params("parallel", "parallel", "arbitrary"), name="seq_dft")(cs, ss, xc, xs)


DFT_SPLIT = 64


def _dft_tables(n):
    k = lax.iota(jnp.int32, n)
    j1 = lax.iota(jnp.int32, n // DFT_SPLIT)
    j0 = lax.iota(jnp.int32, DFT_SPLIT)
    w = 2.0 * math.pi / n
    ang_a = ((j1[:, None] * k[None, :] * DFT_SPLIT) % n).astype(F32) * w
    ang_b = ((j0[:, None] * k[None, :]) % n).astype(F32) * w
    ca, sa = jnp.cos(ang_a)[:, None, :], jnp.sin(ang_a)[:, None, :]
    cb, sb = jnp.cos(ang_b)[None, :, :], jnp.sin(ang_b)[None, :, :]
    sc = n ** -0.5
    cos = ((ca * cb - sa * sb) * sc).reshape(n, n)
    sin = ((sa * cb + ca * sb) * sc).reshape(n, n)
    return cos.astype(BF16), sin.astype(BF16)


def _branch_kernel(a0, a1, a2, w0, w1, w2, g0, g1, g2, m_ref):
    m = (g0[...].astype(F32) * _dot(a0[...], w0[...])
         + g1[...].astype(F32) * _dot(a1[...], w1[...])
         + g2[...].astype(F32) * _dot(a2[...], w2[...]))
    m_ref[...] = m.astype(BF16)


def _branch_merge(a0, a1, a2, w0, w1, w2, gates):
    m = a0.shape[0]
    tm = _tile(m, 1024)
    tn = 512
    nb = D_MODEL // tn
    a_spec = pl.BlockSpec((tm, D_MODEL), lambda i, j: (i, 0))
    w_spec = pl.BlockSpec((D_MODEL, tn), lambda i, j: (0, j))
    g_specs = [pl.BlockSpec((tm, tn), functools.partial(lambda i, j, r: (i, r * nb + j), r=r))
               for r in range(N_BRANCH)]
    return pl.pallas_call(
        _branch_kernel, grid=(m // tm, nb),
        in_specs=[a_spec] * 3 + [w_spec] * 3 + g_specs,
        out_specs=pl.BlockSpec((tm, tn), lambda i, j: (i, j)),
        out_shape=jax.ShapeDtypeStruct((m, D_MODEL), BF16),
        compiler_params=_params("parallel", "arbitrary"), name="branch_merge")(
            a0, a1, a2, w0, w1, w2, gates, gates, gates)


def _proj_ln_kernel(a_ref, w_ref, x_ref, g_ref, b_ref, xo_ref, xb_ref, *, alpha):
    y = alpha * x_ref[...] + _dot(a_ref[...], w_ref[...])
    xn = _layer_norm(y, g_ref[...], b_ref[...])
    xo_ref[...] = xn
    xb_ref[...] = xn.astype(BF16)


def _proj_ln(a, w, x, g, b, alpha):
    m = a.shape[0]
    tm = _tile(m, 512)
    row = pl.BlockSpec((tm, D_MODEL), lambda i: (i, 0))
    vec = pl.BlockSpec((1, D_MODEL), lambda i: (0, 0))
    return pl.pallas_call(
        functools.partial(_proj_ln_kernel, alpha=alpha), grid=(m // tm,),
        in_specs=[row, pl.BlockSpec((D_MODEL, D_MODEL), lambda i: (0, 0)), row, vec, vec],
        out_specs=[row, row],
        out_shape=[jax.ShapeDtypeStruct((m, D_MODEL), F32), jax.ShapeDtypeStruct((m, D_MODEL), BF16)],
        compiler_params=_params("parallel"), name="proj_ln")(a, w, x, g, b)


def _xattn_kernel(q_ref, k_ref, v_ref, o_ref):
    for h in range(X_HEADS):
        sl = slice(h * X_HEAD_DIM, (h + 1) * X_HEAD_DIM)
        s = lax.dot_general(q_ref[:, sl], k_ref[:, sl], (((1,), (1,)), ((), ())),
                            preferred_element_type=F32)
        p = jnp.exp(s - jnp.max(s, axis=-1, keepdims=True))
        l = jnp.sum(p, axis=-1, keepdims=True)
        o_ref[:, sl] = (_dot(p.astype(BF16), v_ref[:, sl]) / l).astype(BF16)


def _xattn(q, km, vm, mem_len, groups):
    m = q.shape[0]
    tq = _tile(min(s for _, s in groups), 512)
    bounds = []
    t0 = r0 = 0
    for rows, s in groups:
        bounds.append((t0, s // tq, r0))
        t0 += rows // tq
        r0 += rows // s

    def mem_block(i):
        blk = jnp.int32(0)
        for first, per, req0 in bounds:
            blk = jnp.where(i >= first, req0 + (i - first) // per, blk)
        return blk

    row = pl.BlockSpec((tq, D_MODEL), lambda i: (i, 0))
    mem = pl.BlockSpec((mem_len, D_MODEL), lambda i: (mem_block(i), 0))
    return pl.pallas_call(
        _xattn_kernel, grid=(m // tq,), in_specs=[row, mem, mem], out_specs=row,
        out_shape=jax.ShapeDtypeStruct((m, D_MODEL), BF16),
        compiler_params=_params("parallel"), name="xattn")(q, km, vm)


def _router_kernel(x_ref, whi_ref, wlo_ref, b_ref, ir_ref, p_ref, cnt_ref, carry_ref):
    i = pl.program_id(0)

    @pl.when(i == 0)
    def _():
        carry_ref[...] = jnp.zeros_like(carry_ref)

    tm = x_ref.shape[0]
    x = x_ref[...]
    x_hi = x.astype(BF16)
    x_lo = (x - x_hi.astype(F32)).astype(BF16)
    logits = (_dot(x_hi, whi_ref[...]) + _dot(x_hi, wlo_ref[...]) + _dot(x_lo, whi_ref[...])
              + b_ref[...])
    lane = lax.broadcasted_iota(jnp.int32, (tm, LANES), 1)
    tops, idxs = [], []
    cur = logits
    for _ in range(TOP_K):
        mx = jnp.max(cur, axis=-1, keepdims=True)
        ix = jnp.min(jnp.where(cur == mx, lane, LANES), axis=-1, keepdims=True)
        tops.append(mx)
        idxs.append(ix)
        cur = jnp.where(lane == ix, NEG_BIG * 2.0, cur)
    exps = [jnp.exp(t - tops[0]) for t in tops]
    denom = exps[0] + exps[1] + exps[2] + exps[3]

    onehot = jnp.zeros((tm, LANES), F32)
    for ix in idxs:
        onehot = onehot + (lane == ix).astype(F32)
    r = lax.broadcasted_iota(jnp.int32, (tm, tm), 0)
    c = lax.broadcasted_iota(jnp.int32, (tm, tm), 1)
    lower = (c < r).astype(BF16)
    before = _dot(lower, onehot.astype(BF16)) + carry_ref[...]

    ir = jnp.zeros((tm, LANES), jnp.int32)
    pr = jnp.zeros((tm, LANES), F32)
    for k in range(TOP_K):
        rank = jnp.sum(jnp.where(lane == idxs[k], before, 0.0), axis=-1, keepdims=True)
        ir = jnp.where(lane == k, idxs[k], ir)
        ir = jnp.where(lane == TOP_K + k, rank.astype(jnp.int32), ir)
        pr = jnp.where(lane == k, exps[k] / denom, pr)
    ir_ref[...] = ir
    p_ref[...] = pr
    carry_ref[...] = carry_ref[...] + jnp.sum(onehot, axis=0, keepdims=True)
    cnt_ref[...] = carry_ref[...]


def _router(x, w_pad, b_pad):
    m = x.shape[0]
    tm = _tile(m, 512)
    w_hi = w_pad.astype(BF16)
    w_lo = (w_pad - w_hi.astype(F32)).astype(BF16)
    return pl.pallas_call(
        _router_kernel, grid=(m // tm,),
        in_specs=[pl.BlockSpec((tm, D_MODEL), lambda i: (i, 0)),
                  pl.BlockSpec((D_MODEL, LANES), lambda i: (0, 0)),
                  pl.BlockSpec((D_MODEL, LANES), lambda i: (0, 0)),
                  pl.BlockSpec((1, LANES), lambda i: (0, 0))],
        out_specs=[pl.BlockSpec((tm, LANES), lambda i: (i, 0)),
                   pl.BlockSpec((tm, LANES), lambda i: (i, 0)),
                   pl.BlockSpec((1, LANES), lambda i: (0, 0))],
        out_shape=[jax.ShapeDtypeStruct((m, LANES), jnp.int32),
                   jax.ShapeDtypeStruct((m, LANES), F32),
                   jax.ShapeDtypeStruct((1, LANES), F32)],
        scratch_shapes=[pltpu.VMEM((1, LANES), F32)],
        compiler_params=_params("arbitrary"), name="router")(x, w_hi, w_lo, b_pad)


def _moe_kernel(te_ref, nu_ref, idx_cur, idx_nxt, x_hbm, wg_ref, wu_ref, bg_ref, bu_ref, wd_ref, bd_ref,
                out_ref, xbuf, xb_ref, sem, *, tm, nf, sub):
    j = pl.program_id(0)
    f = pl.program_id(1)
    n_used = nu_ref[0]

    def start_gather(idx_ref, slot):
        def body(g, carry):
            for u in range(SUBLANES):
                t = idx_ref[0, 0, g * SUBLANES + u]
                pltpu.make_async_copy(
                    x_hbm.at[lax.shift_right_logical(t, 3), pl.ds(lax.bitwise_and(t, SUBLANES - 1), 1), :],
                    xbuf.at[slot, g, pl.ds(u, 1), :], sem.at[slot]).start(priority=GATHER_DMA_PRIORITY)
            return carry
        lax.fori_loop(0, tm // SUBLANES, body, 0)

    @pl.when(jnp.logical_and(j == 0, f == 0))
    def _():
        start_gather(idx_cur, 0)

    @pl.when(j < n_used)
    def _():
        slot = j % 2

        @pl.when(f == 0)
        def _():
            pltpu.make_async_copy(x_hbm.at[pl.ds(0, tm // SUBLANES)], xbuf.at[slot], sem.at[slot]).wait()
            xb_ref[...] = xbuf[slot].reshape(tm, D_MODEL).astype(BF16)

        x = xb_ref[...]
        part = None
        for c in range(wg_ref.shape[3] // sub):
            sl = slice(c * sub, (c + 1) * sub)
            gate = jnp.minimum(_dot(x, wg_ref[0, 0, :, sl]) + bg_ref[0, 0, :, sl], SWIGLU_LIMIT)
            up = jnp.clip(_dot(x, wu_ref[0, 0, :, sl]) + bu_ref[0, 0, :, sl], -SWIGLU_LIMIT, SWIGLU_LIMIT)
            act = (up + 1.0) * gate * jax.nn.sigmoid(SWIGLU_ALPHA * gate)
            d = _dot(act.astype(BF16), wd_ref[0, 0, sl, :])
            part = d if part is None else part + d

        @pl.when(f == 0)
        def _():
            out_ref[...] = part + bd_ref[0, 0]

        @pl.when(f > 0)
        def _():
            out_ref[...] += part

    @pl.when(jnp.logical_and(f == 0, j + 1 < n_used))
    def _():
        start_gather(idx_nxt, (j + 1) % 2)

    @pl.when(jnp.logical_and(j >= n_used, f == nf - 1))
    def _():
        out_ref[...] = jnp.zeros_like(out_ref)


MOE_FF_BLOCK = 1024
MOE_FF_SUB = 512


def _moe_experts(layer, tile_expert, n_used, src_tok, x, w_gu, b_gu, w_down, b_down, tm):
    n_tiles = tile_expert.shape[0]
    tf = MOE_FF_BLOCK
    nf = D_FF // tf
    idx3 = src_tok.reshape(n_tiles, 1, tm)

    def fchunk(j, f, nu):
        snake = jnp.where(j % 2 == 0, f, nf - 1 - f)
        last = jnp.where((nu[0] - 1) % 2 == 0, nf - 1, 0)
        return jnp.where(j < nu[0], snake, last)

    grid_spec = pltpu.PrefetchScalarGridSpec(
        num_scalar_prefetch=2, grid=(n_tiles, nf),
        in_specs=[
            pl.BlockSpec((1, 1, tm), lambda j, f, te, nu: (j, 0, 0), memory_space=pltpu.SMEM),
            pl.BlockSpec((1, 1, tm), lambda j, f, te, nu: (jnp.minimum(j + 1, n_tiles - 1), 0, 0),
                         memory_space=pltpu.SMEM),
            pl.BlockSpec(memory_space=pl.ANY),
            pl.BlockSpec((1, 1, D_MODEL, tf), lambda j, f, te, nu: (layer, te[j], 0, fchunk(j, f, nu))),
            pl.BlockSpec((1, 1, D_MODEL, tf), lambda j, f, te, nu: (layer, te[j], 0, nf + fchunk(j, f, nu))),
            pl.BlockSpec((1, 1, 1, tf), lambda j, f, te, nu: (layer, te[j], 0, fchunk(j, f, nu))),
            pl.BlockSpec((1, 1, 1, tf), lambda j, f, te, nu: (layer, te[j], 0, nf + fchunk(j, f, nu))),
            pl.BlockSpec((1, 1, tf, D_MODEL), lambda j, f, te, nu: (layer, te[j], fchunk(j, f, nu), 0)),
            pl.BlockSpec((1, 1, 1, D_MODEL), lambda j, f, te, nu: (layer, te[j], 0, 0)),
        ],
        out_specs=pl.BlockSpec((tm, D_MODEL), lambda j, f, te, nu: (j, 0)),
        scratch_shapes=[pltpu.VMEM((2, tm // SUBLANES, SUBLANES, D_MODEL), F32),
                        pltpu.VMEM((tm, D_MODEL), BF16),
                        pltpu.SemaphoreType.DMA((2,))])
    return pl.pallas_call(
        functools.partial(_moe_kernel, tm=tm, nf=nf, sub=MOE_FF_SUB), grid_spec=grid_spec,
        out_shape=jax.ShapeDtypeStruct((n_tiles * tm, D_MODEL), F32),
        compiler_params=_params("arbitrary", "arbitrary"), name="moe_experts")(
            tile_expert, n_used, idx3, idx3, x.reshape(-1, SUBLANES, D_MODEL),
            w_gu, w_gu, b_gu, b_gu, w_down, b_down)


def _combine_kernel(pos_cur, pos_nxt, ys_hbm, p_ref, x_ref, g_ref, b_ref, xo_ref, xb_ref, buf, sem,
                    *, tm, alpha):
    i = pl.program_id(0)
    n = pl.num_programs(0)

    def start_gather(pos_ref, slot):
        def body(g, carry):
            for u in range(SUBLANES):
                for k in range(TOP_K):
                    t = pos_ref[0, 0, (g * SUBLANES + u) * TOP_K + k]
                    pltpu.make_async_copy(
                        ys_hbm.at[lax.shift_right_logical(t, 3), pl.ds(lax.bitwise_and(t, SUBLANES - 1), 1), :],
                        buf.at[slot, k, g, pl.ds(u, 1), :], sem.at[slot]).start(priority=k % 2)
            return carry
        lax.fori_loop(0, tm // SUBLANES, body, 0)

    @pl.when(i == 0)
    def _():
        start_gather(pos_cur, 0)

    @pl.when(i + 1 < n)
    def _():
        start_gather(pos_nxt, (i + 1) % 2)

    slot = i % 2
    for k in range(TOP_K):
        pltpu.make_async_copy(ys_hbm.at[pl.ds(0, tm // SUBLANES)], buf.at[slot, k], sem.at[slot]).wait()
    p = p_ref[...]
    y = alpha * x_ref[...]
    for k in range(TOP_K):
        y = y + p[:, k:k + 1] * buf[slot, k].reshape(tm, D_MODEL)
    xn = _layer_norm(y, g_ref[...], b_ref[...])
    xo_ref[...] = xn
    xb_ref[...] = xn.astype(BF16)


def _combine_ln(pos, ys, probs, x, g, b, alpha):
    m = x.shape[0]
    tm = _tile(m, 256)
    nt = m // tm
    pos3 = pos.reshape(nt, 1, tm * TOP_K)
    row = pl.BlockSpec((tm, D_MODEL), lambda i: (i, 0))
    vec = pl.BlockSpec((1, D_MODEL), lambda i: (0, 0))
    return pl.pallas_call(
        functools.partial(_combine_kernel, tm=tm, alpha=alpha), grid=(nt,),
        in_specs=[pl.BlockSpec((1, 1, tm * TOP_K), lambda i: (i, 0, 0), memory_space=pltpu.SMEM),
                  pl.BlockSpec((1, 1, tm * TOP_K), lambda i: (jnp.minimum(i + 1, nt - 1), 0, 0),
                               memory_space=pltpu.SMEM),
                  pl.BlockSpec(memory_space=pl.ANY),
                  pl.BlockSpec((tm, LANES), lambda i: (i, 0)),
                  row, vec, vec],
        out_specs=[row, row],
        out_shape=[jax.ShapeDtypeStruct((m, D_MODEL), F32), jax.ShapeDtypeStruct((m, D_MODEL), BF16)],
        scratch_shapes=[pltpu.VMEM((2, TOP_K, tm // SUBLANES, SUBLANES, D_MODEL), F32),
                        pltpu.SemaphoreType.DMA((2,))],
        compiler_params=_params("arbitrary"), name="combine_ln")(
            pos3, pos3, ys.reshape(-1, SUBLANES, D_MODEL), probs, x, g, b)


def _routing_plan(ir, counts, tm):
    m = ir.shape[0]
    n_tiles = (m * TOP_K) // tm + N_EXPERTS
    idx = ir[:, :TOP_K]
    rank = ir[:, TOP_K:2 * TOP_K]
    cnt = counts[0, :N_EXPERTS].astype(jnp.int32)
    tiles_per = (cnt + tm - 1) // tm
    tile_end = jnp.cumsum(tiles_per)
    tile_start = tile_end - tiles_per
    n_used = tile_end[-1]
    pos = tile_start[idx] * tm + rank
    tile_ids = jnp.arange(n_tiles, dtype=jnp.int32)
    tile_expert = jnp.searchsorted(tile_end, jnp.minimum(tile_ids, n_used - 1), side="right")
    tile_expert = jnp.minimum(tile_expert, N_EXPERTS - 1).astype(jnp.int32)
    tok = jnp.broadcast_to(jnp.arange(m, dtype=jnp.int32)[:, None], (m, TOP_K))
    src_tok = jnp.zeros((n_tiles * tm,), jnp.int32).at[pos.reshape(-1)].set(tok.reshape(-1))
    return pos.astype(jnp.int32), tile_expert, n_used.reshape(1).astype(jnp.int32), src_tok


def _rope_table(groups):
    inv = 1.0 / (ROPE_THETA ** (jnp.arange(0, QK_ROPE, 2, dtype=F32) / QK_ROPE))
    pos = jnp.concatenate([jnp.tile(jnp.arange(s, dtype=F32), rows // s) for rows, s in groups])
    ang = pos[:, None] * inv[None, :]
    cos, sin = jnp.cos(ang), jnp.sin(ang)
    z = jnp.zeros_like(cos)
    return jnp.concatenate([cos, cos, z, z, z, sin, z, z, -sin, z, z, z], axis=1)


def kernel(x_prompt, x_sample, mem_prompt, mem_sample, w_in, b_gate, mla_q_norm, mla_kv_norm, w_uq, w_ukv,
           w_mla_o, sgu_ln_g, sgu_ln_b, sgu_ws, sgu_bs, w_sgu_o, w_fnet_o, w_out, ln1_g, ln1_b, w_cq, w_ck,
           w_cv, w_co, ln2_g, ln2_b, w_router, b_router, w_gu, b_gu, w_down, b_down, ln3_g, ln3_b):
    depth = w_in.shape[0]
    alpha = (2 * depth) ** 0.25
    bp, sp, _ = x_prompt.shape
    bs_, ss, _ = x_sample.shape
    mem_len = mem_prompt.shape[1]
    groups = ((bp * sp, sp), (bs_ * ss, ss))
    assert (bp * sp) % ss == 0 and sp % SGU_CHUNK == 0 and ss % SGU_CHUNK == 0
    m = bp * sp + bs_ * ss

    x = jnp.concatenate([x_prompt.reshape(bp * sp, D_MODEL), x_sample.reshape(bs_ * ss, D_MODEL)], axis=0)
    xb = x.astype(BF16)
    memb = jnp.concatenate([mem_prompt.reshape(bp * mem_len, D_MODEL),
                            mem_sample.reshape(bs_ * mem_len, D_MODEL)], axis=0).astype(BF16)

    tab = _rope_table(groups)
    dft_c, dft_s = _dft_tables(FNET_GW)
    w_chan = jnp.concatenate([dft_c, dft_s], axis=1)
    seq_tabs = {s: _dft_tables(s) for s in {sp, ss}}
    moe_tm = _tile(m * TOP_K, 512)
    w_gu_b = w_gu.astype(BF16)
    w_down_b = w_down.astype(BF16)
    b_gu4 = b_gu.reshape(depth, N_EXPERTS, 1, 2 * D_FF)
    b_down4 = b_down.reshape(depth, N_EXPERTS, 1, D_MODEL)

    for l in range(depth):
        w_lat = jnp.pad(w_in[l, :, :OFF_SGU], ((0, 0), (0, LAT_PAD - OFF_SGU))).astype(BF16)
        w_sgu_in = w_in[l, :, OFF_SGU:OFF_FNET].astype(BF16)
        w_fnet_in = w_in[l, :, OFF_FNET:OFF_GATE].astype(BF16)
        w_gate_in = w_in[l, :, OFF_GATE:].astype(BF16)
        wq = jnp.pad(w_uq[l].reshape(Q_LORA, MLA_HEADS, QK_NOPE + QK_ROPE),
                     ((0, 0), (0, 0), (0, Q_HEAD_PAD - QK_NOPE - QK_ROPE))
                     ).reshape(Q_LORA, MLA_HEADS * Q_HEAD_PAD).astype(BF16)
        bs_full = jnp.repeat(sgu_bs[l].T, SGU_WIDTH // SGU_GROUPS, axis=1)

        cq, ckv, kr = _latent(xb, w_lat, mla_q_norm[l][None], mla_kv_norm[l][None], tab)
        q = _qproj(cq, wq, tab)
        (kv,) = _matmul(ckv, w_ukv[l].astype(BF16), _ep_identity, [BF16], name="kvproj")
        o_parts, row0 = [], 0
        for rows, s in groups:
            o_parts.append(_mla_attention(q, kv, kr, row0, rows // s, s))
            row0 += rows
        o_mla = jnp.concatenate(o_parts, axis=0)

        (zs,) = _matmul(xb, w_sgu_in, _ep_gelu, [BF16], name="sgu_in")
        us = _sgu(zs, sgu_ln_g[l][None], sgu_ln_b[l][None], sgu_ws[l].astype(BF16), bs_full)

        (zf,) = _matmul(xb, w_fnet_in, _ep_identity, [BF16], name="fnet_in")
        xc, xs = _chan_dft(zf, w_chan)
        f_parts, row0 = [], 0
        for rows, s in groups:
            cs, sn = seq_tabs[s]
            f_parts.append(_seq_dft(cs, sn, xc, xs, row0, rows // s, s))
            row0 += rows
        fy = jnp.concatenate(f_parts, axis=0)

        (gates,) = _matmul(xb, w_gate_in, _ep_sigmoid_bias, [BF16], extras=[("col", b_gate[l][None])],
                           name="gates")
        mrg = _branch_merge(o_mla, us, fy, w_mla_o[l].astype(BF16), w_sgu_o[l].astype(BF16),
                            w_fnet_o[l].astype(BF16), gates)
        x, xb = _proj_ln(mrg, w_out[l].astype(BF16), x, ln1_g[l][None], ln1_b[l][None], alpha)

        (qx,) = _matmul(xb, w_cq[l].astype(BF16), functools.partial(_ep_scale, scale=X_HEAD_DIM ** -0.5),
                        [BF16], name="xq")
        (km,) = _matmul(memb, w_ck[l].astype(BF16), _ep_identity, [BF16], name="xk")
        (vm,) = _matmul(memb, w_cv[l].astype(BF16), _ep_identity, [BF16], name="xv")
        ox = _xattn(qx, km, vm, mem_len, groups)
        x, xb = _proj_ln(ox, w_co[l].astype(BF16), x, ln2_g[l][None], ln2_b[l][None], alpha)

        w_r = jnp.pad(w_router[l], ((0, 0), (0, LANES - N_EXPERTS)))
        b_r = jnp.pad(b_router[l], (0, LANES - N_EXPERTS), constant_values=NEG_BIG)[None]
        ir, probs, counts = _router(x, w_r, b_r)
        pos, tile_expert, n_used, src_tok = _routing_plan(ir, counts, moe_tm)
        ys = _moe_experts(l, tile_expert, n_used, src_tok, x, w_gu_b, b_gu4, w_down_b, b_down4, moe_tm)
        x, xb = _combine_ln(pos, ys, probs, x, ln3_g[l][None], ln3_b[l][None], alpha)

    y_prompt = x[:bp * sp].reshape(bp, sp, D_MODEL)
    y_sample = x[bp * sp:].reshape(bs_, ss, D_MODEL)
    return (y_prompt, y_sample)
```

```python
import functools
import math

import jax
import jax.numpy as jnp
import numpy as np
from jax import lax
from jax.experimental import pallas as pl
from jax.experimental.pallas import tpu as pltpu

D_MODEL = 2048
MLA_HEADS = 16
Q_LORA = 512
KV_LORA = 512
QK_NOPE = 128
QK_ROPE = 64
V_HEAD = 128
ROPE_THETA = 10000.0
MLA_SCALE = (QK_NOPE + QK_ROPE) ** -0.5
Q_PRESCALE = MLA_SCALE * math.log2(math.e)
SGU_CHUNK = 128
SGU_GROUPS = 4
SGU_WIDTH = D_MODEL
FNET_GROUPS = 4
FNET_WIDTH = D_MODEL
FNET_GW = FNET_WIDTH // FNET_GROUPS
X_HEADS = 4
X_HEAD_DIM = D_MODEL // X_HEADS
N_EXPERTS = 32
TOP_K = 4
D_FF = D_MODEL
SWIGLU_LIMIT = 7.0
SWIGLU_ALPHA = 1.702
N_BRANCH = 3
LN_EPS = 1e-5
RMS_EPS = 1e-6
OFF_KV = Q_LORA
OFF_KR = OFF_KV + KV_LORA
OFF_SGU = OFF_KR + QK_ROPE
OFF_FNET = OFF_SGU + 2 * SGU_WIDTH
OFF_GATE = OFF_FNET + FNET_WIDTH

LANES = 128
SUBLANES = 8
Q_HEAD_PAD = 2 * LANES
LAT_PAD = OFF_SGU + (LANES - QK_ROPE)
VMEM_LIMIT = 56 * 1024 * 1024
NEG_BIG = -1e30
GATHER_DMA_PRIORITY = 1

BF16 = jnp.bfloat16
F32 = jnp.float32


def _tile(n, pref):
    if n <= pref:
        return n
    t = pref
    while n % t:
        t //= 2
    return t


def _params(*sem):
    return pltpu.CompilerParams(dimension_semantics=sem, vmem_limit_bytes=VMEM_LIMIT)


def _dot(a, b):
    return jnp.dot(a, b, preferred_element_type=F32)


def _layer_norm(x, g, b):
    mu = jnp.mean(x, axis=-1, keepdims=True)
    xc = x - mu
    var = jnp.mean(xc * xc, axis=-1, keepdims=True)
    return xc * lax.rsqrt(var + LN_EPS) * g + b


def _rope_lanes(z, tab):
    return (z * tab[:, :LANES]
            + pltpu.roll(z, QK_ROPE // 2, axis=1) * tab[:, LANES:2 * LANES]
            + pltpu.roll(z, LANES - QK_ROPE // 2, axis=1) * tab[:, 2 * LANES:])


def _mm_kernel(*refs, n_extra, n_out, epilogue):
    a_ref, w_ref = refs[0], refs[1]
    extra = refs[2:2 + n_extra]
    outs = refs[2 + n_extra:2 + n_extra + n_out]
    acc = _dot(a_ref[...], w_ref[...])
    res = epilogue(acc, *[e[...] for e in extra])
    for o, r in zip(outs, res):
        o[...] = r.astype(o.dtype)


def _matmul(a, w, epilogue, out_dtypes, extras=(), tm=1024, tn=1024, name="mm"):
    m, k = a.shape
    n = w.shape[1]
    tm = _tile(m, tm)
    tn = _tile(n, tn)
    in_specs = [pl.BlockSpec((tm, k), lambda i, j: (i, 0)),
                pl.BlockSpec((k, tn), lambda i, j: (0, j))]
    args = [a, w]
    for kind, arr in extras:
        if kind == "col":
            in_specs.append(pl.BlockSpec((1, tn), lambda i, j: (0, j)))
        else:
            in_specs.append(pl.BlockSpec((tm, tn), lambda i, j: (i, j)))
        args.append(arr)
    out_shape = [jax.ShapeDtypeStruct((m, n), dt) for dt in out_dtypes]
    out_specs = [pl.BlockSpec((tm, tn), lambda i, j: (i, j)) for _ in out_dtypes]
    kern = functools.partial(_mm_kernel, n_extra=len(extras), n_out=len(out_dtypes), epilogue=epilogue)
    return pl.pallas_call(
        kern, grid=(m // tm, n // tn), in_specs=in_specs, out_specs=out_specs, out_shape=out_shape,
        compiler_params=_params("parallel", "arbitrary"), name=name)(*args)


def _ep_identity(acc):
    return (acc,)


def _ep_gelu(acc):
    return (0.5 * acc * (1.0 + lax.erf(acc * (2.0 ** -0.5))),)


def _ep_sigmoid_bias(acc, b):
    return (jax.nn.sigmoid(acc + b),)


def _ep_scale(acc, *, scale):
    return (acc * scale,)


def _latent_kernel(x_ref, w_ref, qg_ref, kvg_ref, tab_ref, cq_ref, ckv_ref, kr_ref):
    z = _dot(x_ref[...], w_ref[...])

    def rms(t, g):
        return t * lax.rsqrt(jnp.mean(t * t, axis=-1, keepdims=True) + RMS_EPS) * g

    cq_ref[...] = rms(z[:, :OFF_KV], qg_ref[...]).astype(BF16)
    ckv_ref[...] = rms(z[:, OFF_KV:OFF_KR], kvg_ref[...]).astype(BF16)
    kr_ref[...] = _rope_lanes(z[:, OFF_KR:], tab_ref[...]).astype(BF16)


def _latent(xb, w_lat, qg, kvg, tab):
    m, k = xb.shape
    tm = _tile(m, 512)
    return pl.pallas_call(
        _latent_kernel, grid=(m // tm,),
        in_specs=[pl.BlockSpec((tm, k), lambda i: (i, 0)),
                  pl.BlockSpec((k, LAT_PAD), lambda i: (0, 0)),
                  pl.BlockSpec((1, Q_LORA), lambda i: (0, 0)),
                  pl.BlockSpec((1, KV_LORA), lambda i: (0, 0)),
                  pl.BlockSpec((tm, 3 * LANES), lambda i: (i, 0))],
        out_specs=[pl.BlockSpec((tm, Q_LORA), lambda i: (i, 0)),
                   pl.BlockSpec((tm, KV_LORA), lambda i: (i, 0)),
                   pl.BlockSpec((tm, LANES), lambda i: (i, 0))],
        out_shape=[jax.ShapeDtypeStruct((m, Q_LORA), BF16),
                   jax.ShapeDtypeStruct((m, KV_LORA), BF16),
                   jax.ShapeDtypeStruct((m, LANES), BF16)],
        compiler_params=_params("parallel"), name="latent")(xb, w_lat, qg, kvg, tab)


def _qproj_kernel(c_ref, w_ref, tab_ref, q_ref, *, heads_per_tile):
    acc = _dot(c_ref[...], w_ref[...])
    tab = tab_ref[...]
    for h in range(heads_per_tile):
        lo = h * Q_HEAD_PAD
        q_ref[:, lo:lo + LANES] = (acc[:, lo:lo + LANES] * Q_PRESCALE).astype(BF16)
        q_ref[:, lo + LANES:lo + Q_HEAD_PAD] = (
            _rope_lanes(acc[:, lo + LANES:lo + Q_HEAD_PAD], tab) * Q_PRESCALE).astype(BF16)


def _qproj(cq, wq, tab):
    m, k = cq.shape
    n = wq.shape[1]
    tm = _tile(m, 1024)
    tn = 4 * Q_HEAD_PAD
    return pl.pallas_call(
        functools.partial(_qproj_kernel, heads_per_tile=tn // Q_HEAD_PAD),
        grid=(m // tm, n // tn),
        in_specs=[pl.BlockSpec((tm, k), lambda i, j: (i, 0)),
                  pl.BlockSpec((k, tn), lambda i, j: (0, j)),
                  pl.BlockSpec((tm, 3 * LANES), lambda i, j: (i, 0))],
        out_specs=pl.BlockSpec((tm, tn), lambda i, j: (i, j)),
        out_shape=jax.ShapeDtypeStruct((m, n), BF16),
        compiler_params=_params("parallel", "arbitrary"), name="qproj")(cq, wq, tab)


MLA_HEADS_PER_STEP = 4
MLA_SCORE_SLOTS = 2
MLA_KEY_CHUNK = 512


def _mla_kernel(q_ref, kv_ref, kr_ref, o_ref, kcat_ref, vt_ref, s_ref, *, kc):
    seq = kv_ref.shape[0]
    tq = q_ref.shape[0]
    nck = seq // kc
    nt = (((1,), (1,)), ((), ()))

    @pl.when(pl.program_id(2) == 0)
    def _():
        for h in range(MLA_HEADS_PER_STEP):
            kcat_ref[h, :, :LANES] = kv_ref[:, h * Q_HEAD_PAD:h * Q_HEAD_PAD + LANES]
            kcat_ref[h, :, LANES:] = kr_ref[...]
            for c in range(nck):
                v = kv_ref[c * kc:(c + 1) * kc, h * Q_HEAD_PAD + LANES:(h + 1) * Q_HEAD_PAD]
                vt_ref[h, :, c * kc:(c + 1) * kc] = v.astype(F32).T.astype(BF16)

    for h in range(MLA_HEADS_PER_STEP):
        q = q_ref[:, h * Q_HEAD_PAD:(h + 1) * Q_HEAD_PAD]
        slot = h % MLA_SCORE_SLOTS
        m8 = jnp.full((SUBLANES, tq), -jnp.inf, F32)
        for c in range(nck):
            s_c = lax.dot_general(kcat_ref[h, c * kc:(c + 1) * kc, :], q, nt, preferred_element_type=F32)
            s_ref[slot, c * kc:(c + 1) * kc, :] = s_c
            m8 = jnp.maximum(m8, jnp.max(s_c.reshape(kc // SUBLANES, SUBLANES, tq), axis=0))
        m = jnp.max(m8, axis=0, keepdims=True)
        l8 = jnp.zeros((SUBLANES, tq), F32)
        o_t = jnp.zeros((V_HEAD, tq), F32)
        for c in range(nck):
            p = jnp.exp2(s_ref[slot, c * kc:(c + 1) * kc, :] - m)
            l8 = l8 + jnp.sum(p.reshape(kc // SUBLANES, SUBLANES, tq), axis=0)
            o_t = o_t + _dot(vt_ref[h, :, c * kc:(c + 1) * kc], p.astype(BF16))
        l = jnp.sum(l8, axis=0, keepdims=True)
        o_ref[:, h * V_HEAD:(h + 1) * V_HEAD] = (o_t / l).T.astype(BF16)


def _mla_attention(q, kv, kr, row0, nb, s):
    tq = _tile(s, 512)
    kc = _tile(s, MLA_KEY_CHUNK)
    nq = s // tq
    qb0 = row0 // tq
    sb0 = row0 // s
    hp = MLA_HEADS_PER_STEP
    return pl.pallas_call(
        functools.partial(_mla_kernel, kc=kc), grid=(nb, MLA_HEADS // hp, nq),
        in_specs=[pl.BlockSpec((tq, hp * Q_HEAD_PAD), lambda b, h, i: (qb0 + b * nq + i, h)),
                  pl.BlockSpec((s, hp * Q_HEAD_PAD), lambda b, h, i: (sb0 + b, h)),
                  pl.BlockSpec((s, LANES), lambda b, h, i: (sb0 + b, 0))],
        out_specs=pl.BlockSpec((tq, hp * V_HEAD), lambda b, h, i: (b * nq + i, h)),
        out_shape=jax.ShapeDtypeStruct((nb * s, MLA_HEADS * V_HEAD), BF16),
        scratch_shapes=[pltpu.VMEM((hp, s, Q_HEAD_PAD), BF16), pltpu.VMEM((hp, V_HEAD, s), BF16),
                        pltpu.VMEM((MLA_SCORE_SLOTS, s, tq), F32)],
        compiler_params=_params("parallel", "parallel", "arbitrary"), name="mla_attn")(q, kv, kr)


def _sgu_kernel(u_ref, v_ref, g_ref, b_ref, ws_ref, bs_ref, o_ref, *, chunks):
    v = _layer_norm(v_ref[...].astype(F32), g_ref[...], b_ref[...]).astype(BF16)
    gw = SGU_WIDTH // SGU_GROUPS
    for c in range(chunks):
        r0 = c * SGU_CHUNK
        for g in range(SGU_GROUPS):
            sg = _dot(ws_ref[g], v[r0:r0 + SGU_CHUNK, g * gw:(g + 1) * gw])
            sg = sg + bs_ref[:, g * gw:(g + 1) * gw]
            u = u_ref[r0:r0 + SGU_CHUNK, g * gw:(g + 1) * gw].astype(F32)
            o_ref[r0:r0 + SGU_CHUNK, g * gw:(g + 1) * gw] = (u * sg).astype(BF16)


def _sgu(zs, ln_g, ln_b, ws, bs_full):
    m = zs.shape[0]
    tc = _tile(m, 2 * SGU_CHUNK)
    return pl.pallas_call(
        functools.partial(_sgu_kernel, chunks=tc // SGU_CHUNK), grid=(m // tc,),
        in_specs=[pl.BlockSpec((tc, SGU_WIDTH), lambda i: (i, 0)),
                  pl.BlockSpec((tc, SGU_WIDTH), lambda i: (i, 1)),
                  pl.BlockSpec((1, SGU_WIDTH), lambda i: (0, 0)),
                  pl.BlockSpec((1, SGU_WIDTH), lambda i: (0, 0)),
                  pl.BlockSpec((SGU_GROUPS, SGU_CHUNK, SGU_CHUNK), lambda i: (0, 0, 0)),
                  pl.BlockSpec((SGU_CHUNK, SGU_WIDTH), lambda i: (0, 0))],
        out_specs=pl.BlockSpec((tc, SGU_WIDTH), lambda i: (i, 0)),
        out_shape=jax.ShapeDtypeStruct((m, SGU_WIDTH), BF16),
        compiler_params=_params("parallel"), name="sgu")(zs, zs, ln_g, ln_b, ws, bs_full)


def _chan_dft_kernel(z_ref, w_ref, c_ref, s_ref):
    acc = _dot(z_ref[...], w_ref[...])
    c_ref[...] = acc[:, :FNET_GW].astype(BF16)
    s_ref[...] = acc[:, FNET_GW:].astype(BF16)


def _chan_dft(zf, w_cs):
    m = zf.shape[0]
    tm = _tile(m, 1024)
    blk = pl.BlockSpec((tm, FNET_GW), lambda i, g: (i, g))
    return pl.pallas_call(
        _chan_dft_kernel, grid=(m // tm, FNET_GROUPS),
        in_specs=[blk, pl.BlockSpec((FNET_GW, 2 * FNET_GW), lambda i, g: (0, 0))],
        out_specs=[blk, blk],
        out_shape=[jax.ShapeDtypeStruct((m, FNET_WIDTH), BF16)] * 2,
        compiler_params=_params("parallel", "arbitrary"), name="chan_dft")(zf, w_cs)


def _seq_dft_kernel(cs_ref, ss_ref, xc_ref, xs_ref, y_ref):
    y = _dot(cs_ref[...], xc_ref[...]) - _dot(ss_ref[...], xs_ref[...])
    y_ref[...] = y.astype(BF16)


def _seq_dft(cs, ss, xc, xs, row0, nb, s):
    tm = _tile(s, 512)
    tn = 512
    ni = s // tm
    sb0 = row0 // s
    return pl.pallas_call(
        _seq_dft_kernel, grid=(nb, FNET_WIDTH // tn, ni),
        in_specs=[pl.BlockSpec((tm, s), lambda b, j, i: (i, 0)),
                  pl.BlockSpec((tm, s), lambda b, j, i: (i, 0)),
                  pl.BlockSpec((s, tn), lambda b, j, i: (sb0 + b, j)),
                  pl.BlockSpec((s, tn), lambda b, j, i: (sb0 + b, j))],
        out_specs=pl.BlockSpec((tm, tn), lambda b, j, i: (b * ni + i, j)),
        out_shape=jax.ShapeDtypeStruct((nb * s, FNET_WIDTH), BF16),
        compiler_params=_params("parallel", "parallel", "arbitrary"), name="seq_dft")(cs, ss, xc, xs)


DFT_SPLIT = 64


def _dft_tables(n):
    k = np.arange(n, dtype=np.int64)
    j1 = np.arange(n // DFT_SPLIT, dtype=np.int64)
    j0 = np.arange(DFT_SPLIT, dtype=np.int64)
    w = 2.0 * math.pi / n
    ang_a = ((j1[:, None] * k[None, :] * DFT_SPLIT) % n) * w
    ang_b = ((j0[:, None] * k[None, :]) % n) * w
    ca, sa = (jnp.asarray(t, F32)[:, None, :] for t in (np.cos(ang_a), np.sin(ang_a)))
    cb, sb = (jnp.asarray(t, F32)[None, :, :] for t in (np.cos(ang_b), np.sin(ang_b)))
    sc = n ** -0.5
    cos = ((ca * cb - sa * sb) * sc).reshape(n, n)
    sin = ((sa * cb + ca * sb) * sc).reshape(n, n)
    return cos.astype(BF16), sin.astype(BF16)


def _branch_kernel(a0, a1, a2, w0, w1, w2, g0, g1, g2, m_ref):
    m = (g0[...].astype(F32) * _dot(a0[...], w0[...])
         + g1[...].astype(F32) * _dot(a1[...], w1[...])
         + g2[...].astype(F32) * _dot(a2[...], w2[...]))
    m_ref[...] = m.astype(BF16)


def _branch_merge(a0, a1, a2, w0, w1, w2, gates):
    m = a0.shape[0]
    tm = _tile(m, 1024)
    tn = 512
    nb = D_MODEL // tn
    a_spec = pl.BlockSpec((tm, D_MODEL), lambda i, j: (i, 0))
    w_spec = pl.BlockSpec((D_MODEL, tn), lambda i, j: (0, j))
    g_specs = [pl.BlockSpec((tm, tn), functools.partial(lambda i, j, r: (i, r * nb + j), r=r))
               for r in range(N_BRANCH)]
    return pl.pallas_call(
        _branch_kernel, grid=(m // tm, nb),
        in_specs=[a_spec] * 3 + [w_spec] * 3 + g_specs,
        out_specs=pl.BlockSpec((tm, tn), lambda i, j: (i, j)),
        out_shape=jax.ShapeDtypeStruct((m, D_MODEL), BF16),
        compiler_params=_params("parallel", "arbitrary"), name="branch_merge")(
            a0, a1, a2, w0, w1, w2, gates, gates, gates)


def _proj_ln_kernel(a_ref, w_ref, x_ref, g_ref, b_ref, xo_ref, xb_ref, *, alpha):
    y = alpha * x_ref[...] + _dot(a_ref[...], w_ref[...])
    xn = _layer_norm(y, g_ref[...], b_ref[...])
    xo_ref[...] = xn
    xb_ref[...] = xn.astype(BF16)


def _proj_ln(a, w, x, g, b, alpha):
    m = a.shape[0]
    tm = _tile(m, 512)
    row = pl.BlockSpec((tm, D_MODEL), lambda i: (i, 0))
    vec = pl.BlockSpec((1, D_MODEL), lambda i: (0, 0))
    return pl.pallas_call(
        functools.partial(_proj_ln_kernel, alpha=alpha), grid=(m // tm,),
        in_specs=[row, pl.BlockSpec((D_MODEL, D_MODEL), lambda i: (0, 0)), row, vec, vec],
        out_specs=[row, row],
        out_shape=[jax.ShapeDtypeStruct((m, D_MODEL), F32), jax.ShapeDtypeStruct((m, D_MODEL), BF16)],
        compiler_params=_params("parallel"), name="proj_ln")(a, w, x, g, b)


def _xattn_kernel(q_ref, k_ref, v_ref, o_ref):
    for h in range(X_HEADS):
        sl = slice(h * X_HEAD_DIM, (h + 1) * X_HEAD_DIM)
        s = lax.dot_general(q_ref[:, sl], k_ref[:, sl], (((1,), (1,)), ((), ())),
                            preferred_element_type=F32)
        p = jnp.exp(s - jnp.max(s, axis=-1, keepdims=True))
        l = jnp.sum(p, axis=-1, keepdims=True)
        o_ref[:, sl] = (_dot(p.astype(BF16), v_ref[:, sl]) / l).astype(BF16)


def _xattn(q, km, vm, mem_len, groups):
    m = q.shape[0]
    tq = _tile(min(s for _, s in groups), 512)
    bounds = []
    t0 = r0 = 0
    for rows, s in groups:
        bounds.append((t0, s // tq, r0))
        t0 += rows // tq
        r0 += rows // s

    def mem_block(i):
        blk = jnp.int32(0)
        for first, per, req0 in bounds:
            blk = jnp.where(i >= first, req0 + (i - first) // per, blk)
        return blk

    row = pl.BlockSpec((tq, D_MODEL), lambda i: (i, 0))
    mem = pl.BlockSpec((mem_len, D_MODEL), lambda i: (mem_block(i), 0))
    return pl.pallas_call(
        _xattn_kernel, grid=(m // tq,), in_specs=[row, mem, mem], out_specs=row,
        out_shape=jax.ShapeDtypeStruct((m, D_MODEL), BF16),
        compiler_params=_params("parallel"), name="xattn")(q, km, vm)


def _router_kernel(x_ref, whi_ref, wlo_ref, b_ref, ir_ref, p_ref, cnt_ref, carry_ref):
    i = pl.program_id(0)

    @pl.when(i == 0)
    def _():
        carry_ref[...] = jnp.zeros_like(carry_ref)

    tm = x_ref.shape[0]
    x = x_ref[...]
    x_hi = x.astype(BF16)
    x_lo = (x - x_hi.astype(F32)).astype(BF16)
    logits = (_dot(x_hi, whi_ref[...]) + _dot(x_hi, wlo_ref[...]) + _dot(x_lo, whi_ref[...])
              + b_ref[...])
    lane = lax.broadcasted_iota(jnp.int32, (tm, LANES), 1)
    tops, idxs = [], []
    cur = logits
    for _ in range(TOP_K):
        mx = jnp.max(cur, axis=-1, keepdims=True)
        ix = jnp.min(jnp.where(cur == mx, lane, LANES), axis=-1, keepdims=True)
        tops.append(mx)
        idxs.append(ix)
        cur = jnp.where(lane == ix, NEG_BIG * 2.0, cur)
    exps = [jnp.exp(t - tops[0]) for t in tops]
    denom = exps[0] + exps[1] + exps[2] + exps[3]

    onehot = jnp.zeros((tm, LANES), F32)
    for ix in idxs:
        onehot = onehot + (lane == ix).astype(F32)
    r = lax.broadcasted_iota(jnp.int32, (tm, tm), 0)
    c = lax.broadcasted_iota(jnp.int32, (tm, tm), 1)
    lower = (c < r).astype(BF16)
    before = _dot(lower, onehot.astype(BF16)) + carry_ref[...]

    ir = jnp.zeros((tm, LANES), jnp.int32)
    pr = jnp.zeros((tm, LANES), F32)
    for k in range(TOP_K):
        rank = jnp.sum(jnp.where(lane == idxs[k], before, 0.0), axis=-1, keepdims=True)
        ir = jnp.where(lane == k, idxs[k], ir)
        ir = jnp.where(lane == TOP_K + k, rank.astype(jnp.int32), ir)
        pr = jnp.where(lane == k, exps[k] / denom, pr)
    ir_ref[...] = ir
    p_ref[...] = pr
    carry_ref[...] = carry_ref[...] + jnp.sum(onehot, axis=0, keepdims=True)
    cnt_ref[...] = carry_ref[...]


def _router(x, w_pad, b_pad):
    m = x.shape[0]
    tm = _tile(m, 512)
    w_hi = w_pad.astype(BF16)
    w_lo = (w_pad - w_hi.astype(F32)).astype(BF16)
    return pl.pallas_call(
        _router_kernel, grid=(m // tm,),
        in_specs=[pl.BlockSpec((tm, D_MODEL), lambda i: (i, 0)),
                  pl.BlockSpec((D_MODEL, LANES), lambda i: (0, 0)),
                  pl.BlockSpec((D_MODEL, LANES), lambda i: (0, 0)),
                  pl.BlockSpec((1, LANES), lambda i: (0, 0))],
        out_specs=[pl.BlockSpec((tm, LANES), lambda i: (i, 0)),
                   pl.BlockSpec((tm, LANES), lambda i: (i, 0)),
                   pl.BlockSpec((1, LANES), lambda i: (0, 0))],
        out_shape=[jax.ShapeDtypeStruct((m, LANES), jnp.int32),
                   jax.ShapeDtypeStruct((m, LANES), F32),
                   jax.ShapeDtypeStruct((1, LANES), F32)],
        scratch_shapes=[pltpu.VMEM((1, LANES), F32)],
        compiler_params=_params("arbitrary"), name="router")(x, w_hi, w_lo, b_pad)


def _moe_kernel(te_ref, nu_ref, idx_cur, idx_nxt, x_hbm, wg_ref, wu_ref, bg_ref, bu_ref, wd_ref, bd_ref,
                out_ref, xbuf, xb_ref, sem, *, tm, nf, sub):
    j = pl.program_id(0)
    f = pl.program_id(1)
    n_used = nu_ref[0]

    def start_gather(idx_ref, slot):
        def body(g, carry):
            for u in range(SUBLANES):
                t = idx_ref[0, 0, g * SUBLANES + u]
                pltpu.make_async_copy(
                    x_hbm.at[lax.shift_right_logical(t, 3), pl.ds(lax.bitwise_and(t, SUBLANES - 1), 1), :],
                    xbuf.at[slot, g, pl.ds(u, 1), :], sem.at[slot]).start(priority=GATHER_DMA_PRIORITY)
            return carry
        lax.fori_loop(0, tm // SUBLANES, body, 0)

    @pl.when(jnp.logical_and(jnp.logical_and(j == 0, f == 0), n_used > 0))
    def _():
        start_gather(idx_cur, 0)

    @pl.when(j < n_used)
    def _():
        slot = j % 2

        @pl.when(f == 0)
        def _():
            pltpu.make_async_copy(x_hbm.at[pl.ds(0, tm // SUBLANES)], xbuf.at[slot], sem.at[slot]).wait()
            xb_ref[...] = xbuf[slot].reshape(tm, D_MODEL).astype(BF16)

        x = xb_ref[...]
        part = None
        for c in range(wg_ref.shape[3] // sub):
            sl = slice(c * sub, (c + 1) * sub)
            gate = jnp.minimum(_dot(x, wg_ref[0, 0, :, sl]) + bg_ref[0, 0, :, sl], SWIGLU_LIMIT)
            up = jnp.clip(_dot(x, wu_ref[0, 0, :, sl]) + bu_ref[0, 0, :, sl], -SWIGLU_LIMIT, SWIGLU_LIMIT)
            act = (up + 1.0) * gate * jax.nn.sigmoid(SWIGLU_ALPHA * gate)
            d = _dot(act.astype(BF16), wd_ref[0, 0, sl, :])
            part = d if part is None else part + d

        @pl.when(f == 0)
        def _():
            out_ref[...] = part + bd_ref[0, 0]

        @pl.when(f > 0)
        def _():
            out_ref[...] += part

    @pl.when(jnp.logical_and(f == 0, j + 1 < n_used))
    def _():
        start_gather(idx_nxt, (j + 1) % 2)

    @pl.when(jnp.logical_and(j >= n_used, f == nf - 1))
    def _():
        out_ref[...] = jnp.zeros_like(out_ref)


MOE_FF_BLOCK = 1024
MOE_FF_SUB = 512


def _moe_experts(layer, tile_expert, n_used, src_tok, x, w_gu, b_gu, w_down, b_down, tm):
    n_tiles = tile_expert.shape[0]
    tf = MOE_FF_BLOCK
    nf = D_FF // tf
    idx3 = src_tok.reshape(n_tiles, 1, tm)

    def fchunk(j, f, nu):
        snake = jnp.where(j % 2 == 0, f, nf - 1 - f)
        last = jnp.where((nu[0] - 1) % 2 == 0, nf - 1, 0)
        return jnp.where(j < nu[0], snake, last)

    grid_spec = pltpu.PrefetchScalarGridSpec(
        num_scalar_prefetch=2, grid=(n_tiles, nf),
        in_specs=[
            pl.BlockSpec((1, 1, tm), lambda j, f, te, nu: (j, 0, 0), memory_space=pltpu.SMEM),
            pl.BlockSpec((1, 1, tm), lambda j, f, te, nu: (jnp.minimum(j + 1, n_tiles - 1), 0, 0),
                         memory_space=pltpu.SMEM),
            pl.BlockSpec(memory_space=pl.ANY),
            pl.BlockSpec((1, 1, D_MODEL, tf), lambda j, f, te, nu: (layer, te[j], 0, fchunk(j, f, nu))),
            pl.BlockSpec((1, 1, D_MODEL, tf), lambda j, f, te, nu: (layer, te[j], 0, nf + fchunk(j, f, nu))),
            pl.BlockSpec((1, 1, 1, tf), lambda j, f, te, nu: (layer, te[j], 0, fchunk(j, f, nu))),
            pl.BlockSpec((1, 1, 1, tf), lambda j, f, te, nu: (layer, te[j], 0, nf + fchunk(j, f, nu))),
            pl.BlockSpec((1, 1, tf, D_MODEL), lambda j, f, te, nu: (layer, te[j], fchunk(j, f, nu), 0)),
            pl.BlockSpec((1, 1, 1, D_MODEL), lambda j, f, te, nu: (layer, te[j], 0, 0)),
        ],
        out_specs=pl.BlockSpec((tm, D_MODEL), lambda j, f, te, nu: (j, 0)),
        scratch_shapes=[pltpu.VMEM((2, tm // SUBLANES, SUBLANES, D_MODEL), F32),
                        pltpu.VMEM((tm, D_MODEL), BF16),
                        pltpu.SemaphoreType.DMA((2,))])
    return pl.pallas_call(
        functools.partial(_moe_kernel, tm=tm, nf=nf, sub=MOE_FF_SUB), grid_spec=grid_spec,
        out_shape=jax.ShapeDtypeStruct((n_tiles * tm, D_MODEL), F32),
        compiler_params=_params("arbitrary", "arbitrary"), name="moe_experts")(
            tile_expert, n_used, idx3, idx3, x.reshape(-1, SUBLANES, D_MODEL),
            w_gu, w_gu, b_gu, b_gu, w_down, b_down)


def _combine_kernel(pos_cur, pos_nxt, ys_hbm, p_ref, x_ref, g_ref, b_ref, xo_ref, xb_ref, buf, sem,
                    *, tm, alpha):
    i = pl.program_id(0)
    n = pl.num_programs(0)

    def start_gather(pos_ref, slot):
        def body(g, carry):
            for u in range(SUBLANES):
                for k in range(TOP_K):
                    t = pos_ref[0, 0, (g * SUBLANES + u) * TOP_K + k]
                    pltpu.make_async_copy(
                        ys_hbm.at[lax.shift_right_logical(t, 3), pl.ds(lax.bitwise_and(t, SUBLANES - 1), 1), :],
                        buf.at[slot, k, g, pl.ds(u, 1), :], sem.at[slot]).start(priority=k % 2)
            return carry
        lax.fori_loop(0, tm // SUBLANES, body, 0)

    @pl.when(i == 0)
    def _():
        start_gather(pos_cur, 0)

    @pl.when(i + 1 < n)
    def _():
        start_gather(pos_nxt, (i + 1) % 2)

    slot = i % 2
    for k in range(TOP_K):
        pltpu.make_async_copy(ys_hbm.at[pl.ds(0, tm // SUBLANES)], buf.at[slot, k], sem.at[slot]).wait()
    p = p_ref[...]
    y = alpha * x_ref[...]
    for k in range(TOP_K):
        y = y + p[:, k:k + 1] * buf[slot, k].reshape(tm, D_MODEL)
    xn = _layer_norm(y, g_ref[...], b_ref[...])
    xo_ref[...] = xn
    xb_ref[...] = xn.astype(BF16)


def _combine_ln(pos, ys, probs, x, g, b, alpha):
    m = x.shape[0]
    tm = _tile(m, 256)
    nt = m // tm
    pos3 = pos.reshape(nt, 1, tm * TOP_K)
    row = pl.BlockSpec((tm, D_MODEL), lambda i: (i, 0))
    vec = pl.BlockSpec((1, D_MODEL), lambda i: (0, 0))
    return pl.pallas_call(
        functools.partial(_combine_kernel, tm=tm, alpha=alpha), grid=(nt,),
        in_specs=[pl.BlockSpec((1, 1, tm * TOP_K), lambda i: (i, 0, 0), memory_space=pltpu.SMEM),
                  pl.BlockSpec((1, 1, tm * TOP_K), lambda i: (jnp.minimum(i + 1, nt - 1), 0, 0),
                               memory_space=pltpu.SMEM),
                  pl.BlockSpec(memory_space=pl.ANY),
                  pl.BlockSpec((tm, LANES), lambda i: (i, 0)),
                  row, vec, vec],
        out_specs=[row, row],
        out_shape=[jax.ShapeDtypeStruct((m, D_MODEL), F32), jax.ShapeDtypeStruct((m, D_MODEL), BF16)],
        scratch_shapes=[pltpu.VMEM((2, TOP_K, tm // SUBLANES, SUBLANES, D_MODEL), F32),
                        pltpu.SemaphoreType.DMA((2,))],
        compiler_params=_params("arbitrary"), name="combine_ln")(
            pos3, pos3, ys.reshape(-1, SUBLANES, D_MODEL), probs, x, g, b)


def _routing_plan(ir, counts, tm):
    m = ir.shape[0]
    n_tiles = (m * TOP_K) // tm + N_EXPERTS
    idx = ir[:, :TOP_K]
    rank = ir[:, TOP_K:2 * TOP_K]
    cnt = counts[0, :N_EXPERTS].astype(jnp.int32)
    tiles_per = (cnt + tm - 1) // tm
    tile_end = jnp.cumsum(tiles_per)
    tile_start = tile_end - tiles_per
    n_used = tile_end[-1]
    experts = jnp.arange(N_EXPERTS, dtype=jnp.int32)
    start_of = jnp.sum(jnp.where(idx[:, :, None] == experts, tile_start, 0), axis=-1)
    pos = start_of * tm + rank
    tile_ids = jnp.minimum(jnp.arange(n_tiles, dtype=jnp.int32), n_used - 1)
    tile_expert = jnp.sum((tile_end[None, :] <= tile_ids[:, None]).astype(jnp.int32), axis=1)
    tile_expert = jnp.minimum(tile_expert, N_EXPERTS - 1)
    tok = jnp.broadcast_to(jnp.arange(m, dtype=jnp.int32)[:, None], (m, TOP_K))
    src_tok = jnp.zeros((n_tiles * tm,), jnp.int32).at[pos.reshape(-1)].set(tok.reshape(-1))
    return pos.astype(jnp.int32), tile_expert, n_used.reshape(1).astype(jnp.int32), src_tok


def _rope_table(groups):
    inv = 1.0 / (ROPE_THETA ** (np.arange(0, QK_ROPE, 2, dtype=np.float64) / QK_ROPE))
    parts = []
    for rows, s in groups:
        ang = np.arange(s, dtype=np.float64)[:, None] * inv[None, :]
        cos, sin = np.cos(ang), np.sin(ang)
        z = np.zeros_like(cos)
        base = np.concatenate([cos, cos, z, z, z, sin, z, z, -sin, z, z, z], axis=1).astype(np.float32)
        parts.append(jnp.tile(jnp.asarray(base), (rows // s, 1)))
    return jnp.concatenate(parts, axis=0)


def kernel(x_prompt, x_sample, mem_prompt, mem_sample, w_in, b_gate, mla_q_norm, mla_kv_norm, w_uq, w_ukv,
           w_mla_o, sgu_ln_g, sgu_ln_b, sgu_ws, sgu_bs, w_sgu_o, w_fnet_o, w_out, ln1_g, ln1_b, w_cq, w_ck,
           w_cv, w_co, ln2_g, ln2_b, w_router, b_router, w_gu, b_gu, w_down, b_down, ln3_g, ln3_b):
    depth = w_in.shape[0]
    alpha = (2 * depth) ** 0.25
    bp, sp, _ = x_prompt.shape
    bs_, ss, _ = x_sample.shape
    mem_len = mem_prompt.shape[1]
    groups = ((bp * sp, sp), (bs_ * ss, ss))
    assert (bp * sp) % ss == 0 and sp % SGU_CHUNK == 0 and ss % SGU_CHUNK == 0
    m = bp * sp + bs_ * ss

    x = jnp.concatenate([x_prompt.reshape(bp * sp, D_MODEL), x_sample.reshape(bs_ * ss, D_MODEL)], axis=0)
    xb = x.astype(BF16)
    memb = jnp.concatenate([mem_prompt.reshape(bp * mem_len, D_MODEL),
                            mem_sample.reshape(bs_ * mem_len, D_MODEL)], axis=0).astype(BF16)

    tab = _rope_table(groups)
    dft_c, dft_s = _dft_tables(FNET_GW)
    w_chan = jnp.concatenate([dft_c, dft_s], axis=1)
    seq_tabs = {s: _dft_tables(s) for s in {sp, ss}}
    moe_tm = _tile(m * TOP_K, 512)
    w_gu_b = w_gu.astype(BF16)
    w_down_b = w_down.astype(BF16)
    b_gu4 = b_gu.reshape(depth, N_EXPERTS, 1, 2 * D_FF)
    b_down4 = b_down.reshape(depth, N_EXPERTS, 1, D_MODEL)

    for l in range(depth):
        w_lat = jnp.pad(w_in[l, :, :OFF_SGU], ((0, 0), (0, LAT_PAD - OFF_SGU))).astype(BF16)
        w_sgu_in = w_in[l, :, OFF_SGU:OFF_FNET].astype(BF16)
        w_fnet_in = w_in[l, :, OFF_FNET:OFF_GATE].astype(BF16)
        w_gate_in = w_in[l, :, OFF_GATE:].astype(BF16)
        wq = jnp.pad(w_uq[l].reshape(Q_LORA, MLA_HEADS, QK_NOPE + QK_ROPE),
                     ((0, 0), (0, 0), (0, Q_HEAD_PAD - QK_NOPE - QK_ROPE))
                     ).reshape(Q_LORA, MLA_HEADS * Q_HEAD_PAD).astype(BF16)
        bs_full = jnp.repeat(sgu_bs[l].T, SGU_WIDTH // SGU_GROUPS, axis=1)

        cq, ckv, kr = _latent(xb, w_lat, mla_q_norm[l][None], mla_kv_norm[l][None], tab)
        q = _qproj(cq, wq, tab)
        (kv,) = _matmul(ckv, w_ukv[l].astype(BF16), _ep_identity, [BF16], name="kvproj")
        o_parts, row0 = [], 0
        for rows, s in groups:
            o_parts.append(_mla_attention(q, kv, kr, row0, rows // s, s))
            row0 += rows
        o_mla = jnp.concatenate(o_parts, axis=0)

        (zs,) = _matmul(xb, w_sgu_in, _ep_gelu, [BF16], name="sgu_in")
        us = _sgu(zs, sgu_ln_g[l][None], sgu_ln_b[l][None], sgu_ws[l].astype(BF16), bs_full)

        (zf,) = _matmul(xb, w_fnet_in, _ep_identity, [BF16], name="fnet_in")
        xc, xs = _chan_dft(zf, w_chan)
        f_parts, row0 = [], 0
        for rows, s in groups:
            cs, sn = seq_tabs[s]
            f_parts.append(_seq_dft(cs, sn, xc, xs, row0, rows // s, s))
            row0 += rows
        fy = jnp.concatenate(f_parts, axis=0)

        (gates,) = _matmul(xb, w_gate_in, _ep_sigmoid_bias, [BF16], extras=[("col", b_gate[l][None])],
                           name="gates")
        mrg = _branch_merge(o_mla, us, fy, w_mla_o[l].astype(BF16), w_sgu_o[l].astype(BF16),
                            w_fnet_o[l].astype(BF16), gates)
        x, xb = _proj_ln(mrg, w_out[l].astype(BF16), x, ln1_g[l][None], ln1_b[l][None], alpha)

        (qx,) = _matmul(xb, w_cq[l].astype(BF16), functools.partial(_ep_scale, scale=X_HEAD_DIM ** -0.5),
                        [BF16], name="xq")
        (km,) = _matmul(memb, w_ck[l].astype(BF16), _ep_identity, [BF16], name="xk")
        (vm,) = _matmul(memb, w_cv[l].astype(BF16), _ep_identity, [BF16], name="xv")
        ox = _xattn(qx, km, vm, mem_len, groups)
        x, xb = _proj_ln(ox, w_co[l].astype(BF16), x, ln2_g[l][None], ln2_b[l][None], alpha)

        w_r = jnp.pad(w_router[l], ((0, 0), (0, LANES - N_EXPERTS)))
        b_r = jnp.pad(b_router[l], (0, LANES - N_EXPERTS), constant_values=NEG_BIG)[None]
        ir, probs, counts = _router(x, w_r, b_r)
        pos, tile_expert, n_used, src_tok = _routing_plan(ir, counts, moe_tm)
        ys = _moe_experts(l, tile_expert, n_used, src_tok, x, w_gu_b, b_gu4, w_down_b, b_down4, moe_tm)
        x, xb = _combine_ln(pos, ys, probs, x, ln3_g[l][None], ln3_b[l][None], alpha)

    y_prompt = x[:bp * sp].reshape(bp, sp, D_MODEL)
    y_sample = x[bp * sp:].reshape(bs_, ss, D_MODEL)
    return (y_prompt, y_sample)
```

```python
import functools
import math

import jax
import jax.numpy as jnp
import numpy as np
from jax import lax
from jax.experimental import pallas as pl
from jax.experimental.pallas import tpu as pltpu

D_MODEL = 2048
MLA_HEADS = 16
Q_LORA = 512
KV_LORA = 512
QK_NOPE = 128
QK_ROPE = 64
V_HEAD = 128
ROPE_THETA = 10000.0
MLA_SCALE = (QK_NOPE + QK_ROPE) ** -0.5
Q_PRESCALE = MLA_SCALE * math.log2(math.e)
SGU_CHUNK = 128
SGU_GROUPS = 4
SGU_WIDTH = D_MODEL
FNET_GROUPS = 4
FNET_WIDTH = D_MODEL
FNET_GW = FNET_WIDTH // FNET_GROUPS
X_HEADS = 4
X_HEAD_DIM = D_MODEL // X_HEADS
N_EXPERTS = 32
TOP_K = 4
D_FF = D_MODEL
SWIGLU_LIMIT = 7.0
SWIGLU_ALPHA = 1.702
N_BRANCH = 3
LN_EPS = 1e-5
RMS_EPS = 1e-6
OFF_KV = Q_LORA
OFF_KR = OFF_KV + KV_LORA
OFF_SGU = OFF_KR + QK_ROPE
OFF_FNET = OFF_SGU + 2 * SGU_WIDTH
OFF_GATE = OFF_FNET + FNET_WIDTH

LANES = 128
SUBLANES = 8
Q_HEAD_PAD = 2 * LANES
LAT_PAD = OFF_SGU + (LANES - QK_ROPE)
VMEM_LIMIT = 56 * 1024 * 1024
NEG_BIG = -1e30
GATHER_DMA_PRIORITY = 1

BF16 = jnp.bfloat16
F32 = jnp.float32


def _tile(n, pref):
    if n <= pref:
        return n
    t = pref
    while n % t:
        t //= 2
    return t


def _params(*sem):
    return pltpu.CompilerParams(dimension_semantics=sem, vmem_limit_bytes=VMEM_LIMIT)


def _dot(a, b):
    return jnp.dot(a, b, preferred_element_type=F32)


def _layer_norm(x, g, b):
    mu = jnp.mean(x, axis=-1, keepdims=True)
    xc = x - mu
    var = jnp.mean(xc * xc, axis=-1, keepdims=True)
    return xc * lax.rsqrt(var + LN_EPS) * g + b


def _rope_lanes(z, tab):
    return (z * tab[:, :LANES]
            + pltpu.roll(z, QK_ROPE // 2, axis=1) * tab[:, LANES:2 * LANES]
            + pltpu.roll(z, LANES - QK_ROPE // 2, axis=1) * tab[:, 2 * LANES:])


def _mm_kernel(*refs, n_extra, n_out, epilogue):
    a_ref, w_ref = refs[0], refs[1]
    extra = refs[2:2 + n_extra]
    outs = refs[2 + n_extra:2 + n_extra + n_out]
    acc = _dot(a_ref[...], w_ref[...])
    res = epilogue(acc, *[e[...] for e in extra])
    for o, r in zip(outs, res):
        o[...] = r.astype(o.dtype)


def _matmul(a, w, epilogue, out_dtypes, extras=(), tm=1024, tn=1024, name="mm"):
    m, k = a.shape
    n = w.shape[1]
    tm = _tile(m, tm)
    tn = _tile(n, tn)
    in_specs = [pl.BlockSpec((tm, k), lambda i, j: (i, 0)),
                pl.BlockSpec((k, tn), lambda i, j: (0, j))]
    args = [a, w]
    for kind, arr in extras:
        if kind == "col":
            in_specs.append(pl.BlockSpec((1, tn), lambda i, j: (0, j)))
        else:
            in_specs.append(pl.BlockSpec((tm, tn), lambda i, j: (i, j)))
        args.append(arr)
    out_shape = [jax.ShapeDtypeStruct((m, n), dt) for dt in out_dtypes]
    out_specs = [pl.BlockSpec((tm, tn), lambda i, j: (i, j)) for _ in out_dtypes]
    kern = functools.partial(_mm_kernel, n_extra=len(extras), n_out=len(out_dtypes), epilogue=epilogue)
    return pl.pallas_call(
        kern, grid=(m // tm, n // tn), in_specs=in_specs, out_specs=out_specs, out_shape=out_shape,
        compiler_params=_params("parallel", "arbitrary"), name=name)(*args)


def _ep_identity(acc):
    return (acc,)


def _ep_gelu(acc):
    return (0.5 * acc * (1.0 + lax.erf(acc * (2.0 ** -0.5))),)


def _ep_sigmoid_bias(acc, b):
    return (jax.nn.sigmoid(acc + b),)


def _ep_scale(acc, *, scale):
    return (acc * scale,)


def _latent_kernel(x_ref, w_ref, qg_ref, kvg_ref, tab_ref, cq_ref, ckv_ref, kr_ref):
    z = _dot(x_ref[...], w_ref[...])

    def rms(t, g):
        return t * lax.rsqrt(jnp.mean(t * t, axis=-1, keepdims=True) + RMS_EPS) * g

    cq_ref[...] = rms(z[:, :OFF_KV], qg_ref[...]).astype(BF16)
    ckv_ref[...] = rms(z[:, OFF_KV:OFF_KR], kvg_ref[...]).astype(BF16)
    kr_ref[...] = _rope_lanes(z[:, OFF_KR:], tab_ref[...]).astype(BF16)


def _latent(xb, w_lat, qg, kvg, tab):
    m, k = xb.shape
    tm = _tile(m, 512)
    return pl.pallas_call(
        _latent_kernel, grid=(m // tm,),
        in_specs=[pl.BlockSpec((tm, k), lambda i: (i, 0)),
                  pl.BlockSpec((k, LAT_PAD), lambda i: (0, 0)),
                  pl.BlockSpec((1, Q_LORA), lambda i: (0, 0)),
                  pl.BlockSpec((1, KV_LORA), lambda i: (0, 0)),
                  pl.BlockSpec((tm, 3 * LANES), lambda i: (i, 0))],
        out_specs=[pl.BlockSpec((tm, Q_LORA), lambda i: (i, 0)),
                   pl.BlockSpec((tm, KV_LORA), lambda i: (i, 0)),
                   pl.BlockSpec((tm, LANES), lambda i: (i, 0))],
        out_shape=[jax.ShapeDtypeStruct((m, Q_LORA), BF16),
                   jax.ShapeDtypeStruct((m, KV_LORA), BF16),
                   jax.ShapeDtypeStruct((m, LANES), BF16)],
        compiler_params=_params("parallel"), name="latent")(xb, w_lat, qg, kvg, tab)


def _qproj_kernel(c_ref, w_ref, tab_ref, q_ref, *, heads_per_tile):
    acc = _dot(c_ref[...], w_ref[...])
    tab = tab_ref[...]
    for h in range(heads_per_tile):
        lo = h * Q_HEAD_PAD
        q_ref[:, lo:lo + LANES] = (acc[:, lo:lo + LANES] * Q_PRESCALE).astype(BF16)
        q_ref[:, lo + LANES:lo + Q_HEAD_PAD] = (
            _rope_lanes(acc[:, lo + LANES:lo + Q_HEAD_PAD], tab) * Q_PRESCALE).astype(BF16)


def _qproj(cq, wq, tab):
    m, k = cq.shape
    n = wq.shape[1]
    tm = _tile(m, 1024)
    tn = 4 * Q_HEAD_PAD
    return pl.pallas_call(
        functools.partial(_qproj_kernel, heads_per_tile=tn // Q_HEAD_PAD),
        grid=(m // tm, n // tn),
        in_specs=[pl.BlockSpec((tm, k), lambda i, j: (i, 0)),
                  pl.BlockSpec((k, tn), lambda i, j: (0, j)),
                  pl.BlockSpec((tm, 3 * LANES), lambda i, j: (i, 0))],
        out_specs=pl.BlockSpec((tm, tn), lambda i, j: (i, j)),
        out_shape=jax.ShapeDtypeStruct((m, n), BF16),
        compiler_params=_params("parallel", "arbitrary"), name="qproj")(cq, wq, tab)


MLA_HEADS_PER_STEP = 4
MLA_SCORE_SLOTS = 2
MLA_KEY_CHUNK = 512
MLA_Q_TILE = 512
MLA_SHORT_SEQ = 2048
MLA_Q_TILE_SHORT = 1024


def _mla_kernel(q_ref, kv_ref, kr_ref, o_ref, kcat_ref, vt_ref, s_ref, *, kc):
    seq = kv_ref.shape[0]
    tq = q_ref.shape[0]
    nck = seq // kc
    nt = (((1,), (1,)), ((), ()))

    @pl.when(pl.program_id(2) == 0)
    def _():
        for h in range(MLA_HEADS_PER_STEP):
            kcat_ref[h, :, :LANES] = kv_ref[:, h * Q_HEAD_PAD:h * Q_HEAD_PAD + LANES]
            kcat_ref[h, :, LANES:] = kr_ref[...]
            for c in range(nck):
                v = kv_ref[c * kc:(c + 1) * kc, h * Q_HEAD_PAD + LANES:(h + 1) * Q_HEAD_PAD]
                vt_ref[h, :, c * kc:(c + 1) * kc] = v.astype(F32).T.astype(BF16)

    for h in range(MLA_HEADS_PER_STEP):
        q = q_ref[:, h * Q_HEAD_PAD:(h + 1) * Q_HEAD_PAD]
        slot = h % MLA_SCORE_SLOTS
        m8 = jnp.full((SUBLANES, tq), -jnp.inf, F32)
        for c in range(nck):
            s_c = lax.dot_general(kcat_ref[h, c * kc:(c + 1) * kc, :], q, nt, preferred_element_type=F32)
            s_ref[slot, c * kc:(c + 1) * kc, :] = s_c
            m8 = jnp.maximum(m8, jnp.max(s_c.reshape(kc // SUBLANES, SUBLANES, tq), axis=0))
        m = jnp.max(m8, axis=0, keepdims=True)
        l8 = jnp.zeros((SUBLANES, tq), F32)
        o_t = jnp.zeros((V_HEAD, tq), F32)
        for c in range(nck):
            p = jnp.exp2(s_ref[slot, c * kc:(c + 1) * kc, :] - m)
            l8 = l8 + jnp.sum(p.reshape(kc // SUBLANES, SUBLANES, tq), axis=0)
            o_t = o_t + _dot(vt_ref[h, :, c * kc:(c + 1) * kc], p.astype(BF16))
        l = jnp.sum(l8, axis=0, keepdims=True)
        o_ref[:, h * V_HEAD:(h + 1) * V_HEAD] = (o_t / l).T.astype(BF16)


def _mla_attention(q, kv, kr, row0, nb, s):
    tq = _tile(s, MLA_Q_TILE_SHORT if s <= MLA_SHORT_SEQ else MLA_Q_TILE)
    kc = _tile(s, MLA_KEY_CHUNK)
    nq = s // tq
    qb0 = row0 // tq
    sb0 = row0 // s
    hp = MLA_HEADS_PER_STEP
    return pl.pallas_call(
        functools.partial(_mla_kernel, kc=kc), grid=(nb, MLA_HEADS // hp, nq),
        in_specs=[pl.BlockSpec((tq, hp * Q_HEAD_PAD), lambda b, h, i: (qb0 + b * nq + i, h)),
                  pl.BlockSpec((s, hp * Q_HEAD_PAD), lambda b, h, i: (sb0 + b, h)),
                  pl.BlockSpec((s, LANES), lambda b, h, i: (sb0 + b, 0))],
        out_specs=pl.BlockSpec((tq, hp * V_HEAD), lambda b, h, i: (b * nq + i, h)),
        out_shape=jax.ShapeDtypeStruct((nb * s, MLA_HEADS * V_HEAD), BF16),
        scratch_shapes=[pltpu.VMEM((hp, s, Q_HEAD_PAD), BF16), pltpu.VMEM((hp, V_HEAD, s), BF16),
                        pltpu.VMEM((MLA_SCORE_SLOTS, s, tq), F32)],
        compiler_params=_params("parallel", "parallel", "arbitrary"), name="mla_attn")(q, kv, kr)


def _sgu_kernel(u_ref, v_ref, g_ref, b_ref, ws_ref, bs_ref, o_ref, *, chunks):
    v = _layer_norm(v_ref[...].astype(F32), g_ref[...], b_ref[...]).astype(BF16)
    gw = SGU_WIDTH // SGU_GROUPS
    for c in range(chunks):
        r0 = c * SGU_CHUNK
        for g in range(SGU_GROUPS):
            sg = _dot(ws_ref[g], v[r0:r0 + SGU_CHUNK, g * gw:(g + 1) * gw])
            sg = sg + bs_ref[:, g * gw:(g + 1) * gw]
            u = u_ref[r0:r0 + SGU_CHUNK, g * gw:(g + 1) * gw].astype(F32)
            o_ref[r0:r0 + SGU_CHUNK, g * gw:(g + 1) * gw] = (u * sg).astype(BF16)


def _sgu(zs, ln_g, ln_b, ws, bs_full):
    m = zs.shape[0]
    tc = _tile(m, 2 * SGU_CHUNK)
    return pl.pallas_call(
        functools.partial(_sgu_kernel, chunks=tc // SGU_CHUNK), grid=(m // tc,),
        in_specs=[pl.BlockSpec((tc, SGU_WIDTH), lambda i: (i, 0)),
                  pl.BlockSpec((tc, SGU_WIDTH), lambda i: (i, 1)),
                  pl.BlockSpec((1, SGU_WIDTH), lambda i: (0, 0)),
                  pl.BlockSpec((1, SGU_WIDTH), lambda i: (0, 0)),
                  pl.BlockSpec((SGU_GROUPS, SGU_CHUNK, SGU_CHUNK), lambda i: (0, 0, 0)),
                  pl.BlockSpec((SGU_CHUNK, SGU_WIDTH), lambda i: (0, 0))],
        out_specs=pl.BlockSpec((tc, SGU_WIDTH), lambda i: (i, 0)),
        out_shape=jax.ShapeDtypeStruct((m, SGU_WIDTH), BF16),
        compiler_params=_params("parallel"), name="sgu")(zs, zs, ln_g, ln_b, ws, bs_full)


def _chan_dft_kernel(z_ref, w_ref, c_ref, s_ref):
    acc = _dot(z_ref[...], w_ref[...])
    c_ref[...] = acc[:, :FNET_GW].astype(BF16)
    s_ref[...] = acc[:, FNET_GW:].astype(BF16)


def _chan_dft(zf, w_cs):
    m = zf.shape[0]
    tm = _tile(m, 1024)
    blk = pl.BlockSpec((tm, FNET_GW), lambda i, g: (i, g))
    return pl.pallas_call(
        _chan_dft_kernel, grid=(m // tm, FNET_GROUPS),
        in_specs=[blk, pl.BlockSpec((FNET_GW, 2 * FNET_GW), lambda i, g: (0, 0))],
        out_specs=[blk, blk],
        out_shape=[jax.ShapeDtypeStruct((m, FNET_WIDTH), BF16)] * 2,
        compiler_params=_params("parallel", "arbitrary"), name="chan_dft")(zf, w_cs)


def _seq_dft_kernel(cs_ref, ss_ref, xc_ref, xs_ref, y_ref):
    y = _dot(cs_ref[...], xc_ref[...]) - _dot(ss_ref[...], xs_ref[...])
    y_ref[...] = y.astype(BF16)


def _seq_dft(cs, ss, xc, xs, row0, nb, s):
    tm = _tile(s, 512)
    tn = 512
    ni = s // tm
    sb0 = row0 // s
    return pl.pallas_call(
        _seq_dft_kernel, grid=(nb, FNET_WIDTH // tn, ni),
        in_specs=[pl.BlockSpec((tm, s), lambda b, j, i: (i, 0)),
                  pl.BlockSpec((tm, s), lambda b, j, i: (i, 0)),
                  pl.BlockSpec((s, tn), lambda b, j, i: (sb0 + b, j)),
                  pl.BlockSpec((s, tn), lambda b, j, i: (sb0 + b, j))],
        out_specs=pl.BlockSpec((tm, tn), lambda b, j, i: (b * ni + i, j)),
        out_shape=jax.ShapeDtypeStruct((nb * s, FNET_WIDTH), BF16),
        compiler_params=_params("parallel", "parallel", "arbitrary"), name="seq_dft")(cs, ss, xc, xs)


DFT_SPLIT = 64


def _dft_tables(n):
    k = np.arange(n, dtype=np.int64)
    j1 = np.arange(n // DFT_SPLIT, dtype=np.int64)
    j0 = np.arange(DFT_SPLIT, dtype=np.int64)
    w = 2.0 * math.pi / n
    ang_a = ((j1[:, None] * k[None, :] * DFT_SPLIT) % n) * w
    ang_b = ((j0[:, None] * k[None, :]) % n) * w
    ca, sa = (jnp.asarray(t, F32)[:, None, :] for t in (np.cos(ang_a), np.sin(ang_a)))
    cb, sb = (jnp.asarray(t, F32)[None, :, :] for t in (np.cos(ang_b), np.sin(ang_b)))
    sc = n ** -0.5
    cos = ((ca * cb - sa * sb) * sc).reshape(n, n)
    sin = ((sa * cb + ca * sb) * sc).reshape(n, n)
    return cos.astype(BF16), sin.astype(BF16)


def _branch_kernel(a0, a1, a2, w0, w1, w2, g0, g1, g2, m_ref):
    m = (g0[...].astype(F32) * _dot(a0[...], w0[...])
         + g1[...].astype(F32) * _dot(a1[...], w1[...])
         + g2[...].astype(F32) * _dot(a2[...], w2[...]))
    m_ref[...] = m.astype(BF16)


def _branch_merge(a0, a1, a2, w0, w1, w2, gates):
    m = a0.shape[0]
    tm = _tile(m, 1024)
    tn = 512
    nb = D_MODEL // tn
    a_spec = pl.BlockSpec((tm, D_MODEL), lambda i, j: (i, 0))
    w_spec = pl.BlockSpec((D_MODEL, tn), lambda i, j: (0, j))
    g_specs = [pl.BlockSpec((tm, tn), functools.partial(lambda i, j, r: (i, r * nb + j), r=r))
               for r in range(N_BRANCH)]
    return pl.pallas_call(
        _branch_kernel, grid=(m // tm, nb),
        in_specs=[a_spec] * 3 + [w_spec] * 3 + g_specs,
        out_specs=pl.BlockSpec((tm, tn), lambda i, j: (i, j)),
        out_shape=jax.ShapeDtypeStruct((m, D_MODEL), BF16),
        compiler_params=_params("parallel", "arbitrary"), name="branch_merge")(
            a0, a1, a2, w0, w1, w2, gates, gates, gates)


def _proj_ln_kernel(a_ref, w_ref, x_ref, g_ref, b_ref, xo_ref, xb_ref, *, alpha):
    y = alpha * x_ref[...] + _dot(a_ref[...], w_ref[...])
    xn = _layer_norm(y, g_ref[...], b_ref[...])
    xo_ref[...] = xn
    xb_ref[...] = xn.astype(BF16)


def _proj_ln(a, w, x, g, b, alpha):
    m = a.shape[0]
    tm = _tile(m, 512)
    row = pl.BlockSpec((tm, D_MODEL), lambda i: (i, 0))
    vec = pl.BlockSpec((1, D_MODEL), lambda i: (0, 0))
    return pl.pallas_call(
        functools.partial(_proj_ln_kernel, alpha=alpha), grid=(m // tm,),
        in_specs=[row, pl.BlockSpec((D_MODEL, D_MODEL), lambda i: (0, 0)), row, vec, vec],
        out_specs=[row, row],
        out_shape=[jax.ShapeDtypeStruct((m, D_MODEL), F32), jax.ShapeDtypeStruct((m, D_MODEL), BF16)],
        compiler_params=_params("parallel"), name="proj_ln")(a, w, x, g, b)


def _xattn_kernel(q_ref, k_ref, v_ref, o_ref):
    for h in range(X_HEADS):
        sl = slice(h * X_HEAD_DIM, (h + 1) * X_HEAD_DIM)
        s = lax.dot_general(q_ref[:, sl], k_ref[:, sl], (((1,), (1,)), ((), ())),
                            preferred_element_type=F32)
        p = jnp.exp(s - jnp.max(s, axis=-1, keepdims=True))
        l = jnp.sum(p, axis=-1, keepdims=True)
        o_ref[:, sl] = (_dot(p.astype(BF16), v_ref[:, sl]) / l).astype(BF16)


def _xattn(q, km, vm, mem_len, groups):
    m = q.shape[0]
    tq = _tile(min(s for _, s in groups), 512)
    bounds = []
    t0 = r0 = 0
    for rows, s in groups:
        bounds.append((t0, s // tq, r0))
        t0 += rows // tq
        r0 += rows // s

    def mem_block(i):
        blk = jnp.int32(0)
        for first, per, req0 in bounds:
            blk = jnp.where(i >= first, req0 + (i - first) // per, blk)
        return blk

    row = pl.BlockSpec((tq, D_MODEL), lambda i: (i, 0))
    mem = pl.BlockSpec((mem_len, D_MODEL), lambda i: (mem_block(i), 0))
    return pl.pallas_call(
        _xattn_kernel, grid=(m // tq,), in_specs=[row, mem, mem], out_specs=row,
        out_shape=jax.ShapeDtypeStruct((m, D_MODEL), BF16),
        compiler_params=_params("parallel"), name="xattn")(q, km, vm)


def _router_kernel(x_ref, whi_ref, wlo_ref, b_ref, ir_ref, p_ref, cnt_ref, carry_ref):
    i = pl.program_id(0)

    @pl.when(i == 0)
    def _():
        carry_ref[...] = jnp.zeros_like(carry_ref)

    tm = x_ref.shape[0]
    x = x_ref[...]
    x_hi = x.astype(BF16)
    x_lo = (x - x_hi.astype(F32)).astype(BF16)
    logits = (_dot(x_hi, whi_ref[...]) + _dot(x_hi, wlo_ref[...]) + _dot(x_lo, whi_ref[...])
              + b_ref[...])
    lane = lax.broadcasted_iota(jnp.int32, (tm, LANES), 1)
    tops, idxs = [], []
    cur = logits
    for _ in range(TOP_K):
        mx = jnp.max(cur, axis=-1, keepdims=True)
        ix = jnp.min(jnp.where(cur == mx, lane, LANES), axis=-1, keepdims=True)
        tops.append(mx)
        idxs.append(ix)
        cur = jnp.where(lane == ix, NEG_BIG * 2.0, cur)
    exps = [jnp.exp(t - tops[0]) for t in tops]
    denom = exps[0] + exps[1] + exps[2] + exps[3]

    onehot = jnp.zeros((tm, LANES), F32)
    for ix in idxs:
        onehot = onehot + (lane == ix).astype(F32)
    r = lax.broadcasted_iota(jnp.int32, (tm, tm), 0)
    c = lax.broadcasted_iota(jnp.int32, (tm, tm), 1)
    lower = (c < r).astype(BF16)
    before = _dot(lower, onehot.astype(BF16)) + carry_ref[...]

    ir = jnp.zeros((tm, LANES), jnp.int32)
    pr = jnp.zeros((tm, LANES), F32)
    for k in range(TOP_K):
        rank = jnp.sum(jnp.where(lane == idxs[k], before, 0.0), axis=-1, keepdims=True)
        ir = jnp.where(lane == k, idxs[k], ir)
        ir = jnp.where(lane == TOP_K + k, rank.astype(jnp.int32), ir)
        pr = jnp.where(lane == k, exps[k] / denom, pr)
    ir_ref[...] = ir
    p_ref[...] = pr
    carry_ref[...] = carry_ref[...] + jnp.sum(onehot, axis=0, keepdims=True)
    cnt_ref[...] = carry_ref[...]


def _router(x, w_pad, b_pad):
    m = x.shape[0]
    tm = _tile(m, 512)
    w_hi = w_pad.astype(BF16)
    w_lo = (w_pad - w_hi.astype(F32)).astype(BF16)
    return pl.pallas_call(
        _router_kernel, grid=(m // tm,),
        in_specs=[pl.BlockSpec((tm, D_MODEL), lambda i: (i, 0)),
                  pl.BlockSpec((D_MODEL, LANES), lambda i: (0, 0)),
                  pl.BlockSpec((D_MODEL, LANES), lambda i: (0, 0)),
                  pl.BlockSpec((1, LANES), lambda i: (0, 0))],
        out_specs=[pl.BlockSpec((tm, LANES), lambda i: (i, 0)),
                   pl.BlockSpec((tm, LANES), lambda i: (i, 0)),
                   pl.BlockSpec((1, LANES), lambda i: (0, 0))],
        out_shape=[jax.ShapeDtypeStruct((m, LANES), jnp.int32),
                   jax.ShapeDtypeStruct((m, LANES), F32),
                   jax.ShapeDtypeStruct((1, LANES), F32)],
        scratch_shapes=[pltpu.VMEM((1, LANES), F32)],
        compiler_params=_params("arbitrary"), name="router")(x, w_hi, w_lo, b_pad)


def _moe_kernel(te_ref, nu_ref, idx_cur, idx_nxt, x_hbm, wg_ref, wu_ref, bg_ref, bu_ref, wd_ref, bd_ref,
                out_ref, xbuf, xb_ref, sem, *, tm, nf, sub):
    j = pl.program_id(0)
    f = pl.program_id(1)
    n_used = nu_ref[0]

    def start_gather(idx_ref, slot):
        def body(g, carry):
            for u in range(SUBLANES):
                t = idx_ref[0, 0, g * SUBLANES + u]
                pltpu.make_async_copy(
                    x_hbm.at[lax.shift_right_logical(t, 3), pl.ds(lax.bitwise_and(t, SUBLANES - 1), 1), :],
                    xbuf.at[slot, g, pl.ds(u, 1), :], sem.at[slot]).start(priority=GATHER_DMA_PRIORITY)
            return carry
        lax.fori_loop(0, tm // SUBLANES, body, 0)

    @pl.when(jnp.logical_and(jnp.logical_and(j == 0, f == 0), n_used > 0))
    def _():
        start_gather(idx_cur, 0)

    @pl.when(j < n_used)
    def _():
        slot = j % 2

        @pl.when(f == 0)
        def _():
            pltpu.make_async_copy(x_hbm.at[pl.ds(0, tm // SUBLANES)], xbuf.at[slot], sem.at[slot]).wait()
            xb_ref[...] = xbuf[slot].reshape(tm, D_MODEL).astype(BF16)

        x = xb_ref[...]
        part = None
        for c in range(wg_ref.shape[3] // sub):
            sl = slice(c * sub, (c + 1) * sub)
            gate = jnp.minimum(_dot(x, wg_ref[0, 0, :, sl]) + bg_ref[0, 0, :, sl], SWIGLU_LIMIT)
            up = jnp.clip(_dot(x, wu_ref[0, 0, :, sl]) + bu_ref[0, 0, :, sl], -SWIGLU_LIMIT, SWIGLU_LIMIT)
            act = (up + 1.0) * gate * jax.nn.sigmoid(SWIGLU_ALPHA * gate)
            d = _dot(act.astype(BF16), wd_ref[0, 0, sl, :])
            part = d if part is None else part + d

        @pl.when(f == 0)
        def _():
            out_ref[...] = part + bd_ref[0, 0]

        @pl.when(f > 0)
        def _():
            out_ref[...] += part

    @pl.when(jnp.logical_and(f == 0, j + 1 < n_used))
    def _():
        start_gather(idx_nxt, (j + 1) % 2)

    @pl.when(jnp.logical_and(j >= n_used, f == nf - 1))
    def _():
        out_ref[...] = jnp.zeros_like(out_ref)


MOE_FF_BLOCK = 1024
MOE_FF_SUB = 512


def _moe_experts(layer, tile_expert, n_used, src_tok, x, w_gu, b_gu, w_down, b_down, tm):
    n_tiles = tile_expert.shape[0]
    tf = MOE_FF_BLOCK
    nf = D_FF // tf
    idx3 = src_tok.reshape(n_tiles, 1, tm)

    def fchunk(j, f, nu):
        snake = jnp.where(j % 2 == 0, f, nf - 1 - f)
        last = jnp.where((nu[0] - 1) % 2 == 0, nf - 1, 0)
        return jnp.where(j < nu[0], snake, last)

    grid_spec = pltpu.PrefetchScalarGridSpec(
        num_scalar_prefetch=2, grid=(n_tiles, nf),
        in_specs=[
            pl.BlockSpec((1, 1, tm), lambda j, f, te, nu: (j, 0, 0), memory_space=pltpu.SMEM),
            pl.BlockSpec((1, 1, tm), lambda j, f, te, nu: (jnp.minimum(j + 1, n_tiles - 1), 0, 0),
                         memory_space=pltpu.SMEM),
            pl.BlockSpec(memory_space=pl.ANY),
            pl.BlockSpec((1, 1, D_MODEL, tf), lambda j, f, te, nu: (layer, te[j], 0, fchunk(j, f, nu))),
            pl.BlockSpec((1, 1, D_MODEL, tf), lambda j, f, te, nu: (layer, te[j], 0, nf + fchunk(j, f, nu))),
            pl.BlockSpec((1, 1, 1, tf), lambda j, f, te, nu: (layer, te[j], 0, fchunk(j, f, nu))),
            pl.BlockSpec((1, 1, 1, tf), lambda j, f, te, nu: (layer, te[j], 0, nf + fchunk(j, f, nu))),
            pl.BlockSpec((1, 1, tf, D_MODEL), lambda j, f, te, nu: (layer, te[j], fchunk(j, f, nu), 0)),
            pl.BlockSpec((1, 1, 1, D_MODEL), lambda j, f, te, nu: (layer, te[j], 0, 0)),
        ],
        out_specs=pl.BlockSpec((tm, D_MODEL), lambda j, f, te, nu: (j, 0)),
        scratch_shapes=[pltpu.VMEM((2, tm // SUBLANES, SUBLANES, D_MODEL), F32),
                        pltpu.VMEM((tm, D_MODEL), BF16),
                        pltpu.SemaphoreType.DMA((2,))])
    return pl.pallas_call(
        functools.partial(_moe_kernel, tm=tm, nf=nf, sub=MOE_FF_SUB), grid_spec=grid_spec,
        out_shape=jax.ShapeDtypeStruct((n_tiles * tm, D_MODEL), F32),
        compiler_params=_params("arbitrary", "arbitrary"), name="moe_experts")(
            tile_expert, n_used, idx3, idx3, x.reshape(-1, SUBLANES, D_MODEL),
            w_gu, w_gu, b_gu, b_gu, w_down, b_down)


def _combine_kernel(pos_cur, pos_nxt, ys_hbm, p_ref, x_ref, g_ref, b_ref, xo_ref, xb_ref, buf, sem,
                    *, tm, alpha, first_tiles):
    i = pl.program_id(0)
    n = pl.num_programs(0)

    def start_gather(pos_ref, slot):
        def body(g, carry):
            for u in range(SUBLANES):
                for k in range(TOP_K):
                    t = pos_ref[0, 0, (g * SUBLANES + u) * TOP_K + k]
                    pltpu.make_async_copy(
                        ys_hbm.at[lax.shift_right_logical(t, 3), pl.ds(lax.bitwise_and(t, SUBLANES - 1), 1), :],
                        buf.at[slot, k, g, pl.ds(u, 1), :], sem.at[slot]).start(priority=k % 2)
            return carry
        lax.fori_loop(0, tm // SUBLANES, body, 0)

    @pl.when(i == 0)
    def _():
        start_gather(pos_cur, 0)

    @pl.when(i + 1 < n)
    def _():
        start_gather(pos_nxt, (i + 1) % 2)

    slot = i % 2
    for k in range(TOP_K):
        pltpu.make_async_copy(ys_hbm.at[pl.ds(0, tm // SUBLANES)], buf.at[slot, k], sem.at[slot]).wait()
    p = p_ref[...]
    y = alpha * x_ref[...]
    for k in range(TOP_K):
        y = y + p[:, k:k + 1] * buf[slot, k].reshape(tm, D_MODEL)
    xn = _layer_norm(y, g_ref[...], b_ref[...])
    if first_tiles is None:
        xo_ref[...] = xn
        xb_ref[...] = xn.astype(BF16)
    else:
        @pl.when(i < first_tiles)
        def _():
            xo_ref[...] = xn

        @pl.when(i >= first_tiles)
        def _():
            xb_ref[...] = xn


def _combine_ln(pos, ys, probs, x, g, b, alpha, split_rows=None):
    m = x.shape[0]
    tm = _tile(m, 256) if split_rows is None else _tile(math.gcd(split_rows, m - split_rows), 256)
    nt = m // tm
    pos3 = pos.reshape(nt, 1, tm * TOP_K)
    row = pl.BlockSpec((tm, D_MODEL), lambda i: (i, 0))
    vec = pl.BlockSpec((1, D_MODEL), lambda i: (0, 0))
    if split_rows is None:
        first_tiles = None
        out_specs = [row, row]
        out_shape = [jax.ShapeDtypeStruct((m, D_MODEL), F32), jax.ShapeDtypeStruct((m, D_MODEL), BF16)]
    else:
        first_tiles = split_rows // tm
        out_specs = [pl.BlockSpec((tm, D_MODEL), lambda i: (jnp.minimum(i, first_tiles - 1), 0)),
                     pl.BlockSpec((tm, D_MODEL), lambda i: (jnp.maximum(i - first_tiles, 0), 0))]
        out_shape = [jax.ShapeDtypeStruct((split_rows, D_MODEL), F32),
                     jax.ShapeDtypeStruct((m - split_rows, D_MODEL), F32)]
    return pl.pallas_call(
        functools.partial(_combine_kernel, tm=tm, alpha=alpha, first_tiles=first_tiles), grid=(nt,),
        in_specs=[pl.BlockSpec((1, 1, tm * TOP_K), lambda i: (i, 0, 0), memory_space=pltpu.SMEM),
                  pl.BlockSpec((1, 1, tm * TOP_K), lambda i: (jnp.minimum(i + 1, nt - 1), 0, 0),
                               memory_space=pltpu.SMEM),
                  pl.BlockSpec(memory_space=pl.ANY),
                  pl.BlockSpec((tm, LANES), lambda i: (i, 0)),
                  row, vec, vec],
        out_specs=out_specs, out_shape=out_shape,
        scratch_shapes=[pltpu.VMEM((2, TOP_K, tm // SUBLANES, SUBLANES, D_MODEL), F32),
                        pltpu.SemaphoreType.DMA((2,))],
        compiler_params=_params("arbitrary"), name="combine_ln")(
            pos3, pos3, ys.reshape(-1, SUBLANES, D_MODEL), probs, x, g, b)


def _routing_plan(ir, counts, tm):
    m = ir.shape[0]
    n_tiles = (m * TOP_K) // tm + N_EXPERTS
    idx = ir[:, :TOP_K]
    rank = ir[:, TOP_K:2 * TOP_K]
    cnt = counts[0, :N_EXPERTS].astype(jnp.int32)
    tiles_per = (cnt + tm - 1) // tm
    tile_end = jnp.cumsum(tiles_per)
    tile_start = tile_end - tiles_per
    n_used = tile_end[-1]
    experts = jnp.arange(N_EXPERTS, dtype=jnp.int32)
    start_of = jnp.sum(jnp.where(idx[:, :, None] == experts, tile_start, 0), axis=-1)
    pos = start_of * tm + rank
    tile_ids = jnp.minimum(jnp.arange(n_tiles, dtype=jnp.int32), n_used - 1)
    tile_expert = jnp.sum((tile_end[None, :] <= tile_ids[:, None]).astype(jnp.int32), axis=1)
    tile_expert = jnp.minimum(tile_expert, N_EXPERTS - 1)
    tok = jnp.broadcast_to(jnp.arange(m, dtype=jnp.int32)[:, None], (m, TOP_K))
    src_tok = jnp.zeros((n_tiles * tm,), jnp.int32).at[pos.reshape(-1)].set(tok.reshape(-1))
    return pos.astype(jnp.int32), tile_expert, n_used.reshape(1).astype(jnp.int32), src_tok


def _rope_table(groups):
    inv = 1.0 / (ROPE_THETA ** (np.arange(0, QK_ROPE, 2, dtype=np.float64) / QK_ROPE))
    parts = []
    for rows, s in groups:
        ang = np.arange(s, dtype=np.float64)[:, None] * inv[None, :]
        cos, sin = np.cos(ang), np.sin(ang)
        z = np.zeros_like(cos)
        base = np.concatenate([cos, cos, z, z, z, sin, z, z, -sin, z, z, z], axis=1).astype(np.float32)
        parts.append(jnp.tile(jnp.asarray(base), (rows // s, 1)))
    return jnp.concatenate(parts, axis=0)


def kernel(x_prompt, x_sample, mem_prompt, mem_sample, w_in, b_gate, mla_q_norm, mla_kv_norm, w_uq, w_ukv,
           w_mla_o, sgu_ln_g, sgu_ln_b, sgu_ws, sgu_bs, w_sgu_o, w_fnet_o, w_out, ln1_g, ln1_b, w_cq, w_ck,
           w_cv, w_co, ln2_g, ln2_b, w_router, b_router, w_gu, b_gu, w_down, b_down, ln3_g, ln3_b):
    depth = w_in.shape[0]
    alpha = (2 * depth) ** 0.25
    bp, sp, _ = x_prompt.shape
    bs_, ss, _ = x_sample.shape
    mem_len = mem_prompt.shape[1]
    groups = ((bp * sp, sp), (bs_ * ss, ss))
    assert (bp * sp) % ss == 0 and sp % SGU_CHUNK == 0 and ss % SGU_CHUNK == 0
    m = bp * sp + bs_ * ss

    x = jnp.concatenate([x_prompt.reshape(bp * sp, D_MODEL), x_sample.reshape(bs_ * ss, D_MODEL)], axis=0)
    xb = x.astype(BF16)
    memb = jnp.concatenate([mem_prompt.reshape(bp * mem_len, D_MODEL),
                            mem_sample.reshape(bs_ * mem_len, D_MODEL)], axis=0).astype(BF16)

    tab = _rope_table(groups)
    dft_c, dft_s = _dft_tables(FNET_GW)
    w_chan = jnp.concatenate([dft_c, dft_s], axis=1)
    seq_tabs = {s: _dft_tables(s) for s in {sp, ss}}
    moe_tm = _tile(m * TOP_K, 512)
    w_gu_b = w_gu.astype(BF16)
    w_down_b = w_down.astype(BF16)
    b_gu4 = b_gu.reshape(depth, N_EXPERTS, 1, 2 * D_FF)
    b_down4 = b_down.reshape(depth, N_EXPERTS, 1, D_MODEL)

    for l in range(depth):
        w_lat = jnp.pad(w_in[l, :, :OFF_SGU], ((0, 0), (0, LAT_PAD - OFF_SGU))).astype(BF16)
        w_sgu_in = w_in[l, :, OFF_SGU:OFF_FNET].astype(BF16)
        w_fnet_in = w_in[l, :, OFF_FNET:OFF_GATE].astype(BF16)
        w_gate_in = w_in[l, :, OFF_GATE:].astype(BF16)
        wq = jnp.pad(w_uq[l].reshape(Q_LORA, MLA_HEADS, QK_NOPE + QK_ROPE),
                     ((0, 0), (0, 0), (0, Q_HEAD_PAD - QK_NOPE - QK_ROPE))
                     ).reshape(Q_LORA, MLA_HEADS * Q_HEAD_PAD).astype(BF16)
        bs_full = jnp.repeat(sgu_bs[l].T, SGU_WIDTH // SGU_GROUPS, axis=1)

        cq, ckv, kr = _latent(xb, w_lat, mla_q_norm[l][None], mla_kv_norm[l][None], tab)
        q = _qproj(cq, wq, tab)
        (kv,) = _matmul(ckv, w_ukv[l].astype(BF16), _ep_identity, [BF16], name="kvproj")
        o_parts, row0 = [], 0
        for rows, s in groups:
            o_parts.append(_mla_attention(q, kv, kr, row0, rows // s, s))
            row0 += rows
        o_mla = jnp.concatenate(o_parts, axis=0)

        (zs,) = _matmul(xb, w_sgu_in, _ep_gelu, [BF16], name="sgu_in")
        us = _sgu(zs, sgu_ln_g[l][None], sgu_ln_b[l][None], sgu_ws[l].astype(BF16), bs_full)

        (zf,) = _matmul(xb, w_fnet_in, _ep_identity, [BF16], name="fnet_in")
        xc, xs = _chan_dft(zf, w_chan)
        f_parts, row0 = [], 0
        for rows, s in groups:
            cs, sn = seq_tabs[s]
            f_parts.append(_seq_dft(cs, sn, xc, xs, row0, rows // s, s))
            row0 += rows
        fy = jnp.concatenate(f_parts, axis=0)

        (gates,) = _matmul(xb, w_gate_in, _ep_sigmoid_bias, [BF16], extras=[("col", b_gate[l][None])],
                           name="gates")
        mrg = _branch_merge(o_mla, us, fy, w_mla_o[l].astype(BF16), w_sgu_o[l].astype(BF16),
                            w_fnet_o[l].astype(BF16), gates)
        x, xb = _proj_ln(mrg, w_out[l].astype(BF16), x, ln1_g[l][None], ln1_b[l][None], alpha)

        (qx,) = _matmul(xb, w_cq[l].astype(BF16), functools.partial(_ep_scale, scale=X_HEAD_DIM ** -0.5),
                        [BF16], name="xq")
        (km,) = _matmul(memb, w_ck[l].astype(BF16), _ep_identity, [BF16], name="xk")
        (vm,) = _matmul(memb, w_cv[l].astype(BF16), _ep_identity, [BF16], name="xv")
        ox = _xattn(qx, km, vm, mem_len, groups)
        x, xb = _proj_ln(ox, w_co[l].astype(BF16), x, ln2_g[l][None], ln2_b[l][None], alpha)

        w_r = jnp.pad(w_router[l], ((0, 0), (0, LANES - N_EXPERTS)))
        b_r = jnp.pad(b_router[l], (0, LANES - N_EXPERTS), constant_values=NEG_BIG)[None]
        ir, probs, counts = _router(x, w_r, b_r)
        pos, tile_expert, n_used, src_tok = _routing_plan(ir, counts, moe_tm)
        ys = _moe_experts(l, tile_expert, n_used, src_tok, x, w_gu_b, b_gu4, w_down_b, b_down4, moe_tm)
        if l + 1 < depth:
            x, xb = _combine_ln(pos, ys, probs, x, ln3_g[l][None], ln3_b[l][None], alpha)
        else:
            y_p, y_s = _combine_ln(pos, ys, probs, x, ln3_g[l][None], ln3_b[l][None], alpha,
                                   split_rows=bp * sp)
    return (y_p.reshape(bp, sp, D_MODEL), y_s.reshape(bs_, ss, D_MODEL))
```

```python
import functools
import math

import jax
import jax.numpy as jnp
import numpy as np
from jax import lax
from jax.experimental import pallas as pl
from jax.experimental.pallas import tpu as pltpu

D_MODEL = 2048
MLA_HEADS = 16
Q_LORA = 512
KV_LORA = 512
QK_NOPE = 128
QK_ROPE = 64
V_HEAD = 128
ROPE_THETA = 10000.0
MLA_SCALE = (QK_NOPE + QK_ROPE) ** -0.5
Q_PRESCALE = MLA_SCALE * math.log2(math.e)
SGU_CHUNK = 128
SGU_GROUPS = 4
SGU_WIDTH = D_MODEL
FNET_GROUPS = 4
FNET_WIDTH = D_MODEL
FNET_GW = FNET_WIDTH // FNET_GROUPS
X_HEADS = 4
X_HEAD_DIM = D_MODEL // X_HEADS
N_EXPERTS = 32
TOP_K = 4
D_FF = D_MODEL
SWIGLU_LIMIT = 7.0
SWIGLU_ALPHA = 1.702
N_BRANCH = 3
LN_EPS = 1e-5
RMS_EPS = 1e-6
OFF_KV = Q_LORA
OFF_KR = OFF_KV + KV_LORA
OFF_SGU = OFF_KR + QK_ROPE
OFF_FNET = OFF_SGU + 2 * SGU_WIDTH
OFF_GATE = OFF_FNET + FNET_WIDTH

LANES = 128
SUBLANES = 8
Q_HEAD_PAD = 2 * LANES
LAT_PAD = OFF_SGU + (LANES - QK_ROPE)
VMEM_LIMIT = 56 * 1024 * 1024
NEG_BIG = -1e30
GATHER_DMA_PRIORITY = 1

BF16 = jnp.bfloat16
F32 = jnp.float32


def _tile(n, pref):
    if n <= pref:
        return n
    t = pref
    while n % t:
        t //= 2
    return t


def _params(*sem):
    return pltpu.CompilerParams(dimension_semantics=sem, vmem_limit_bytes=VMEM_LIMIT)


def _dot(a, b):
    return jnp.dot(a, b, preferred_element_type=F32)


def _layer_norm(x, g, b):
    mu = jnp.mean(x, axis=-1, keepdims=True)
    xc = x - mu
    var = jnp.mean(xc * xc, axis=-1, keepdims=True)
    return xc * lax.rsqrt(var + LN_EPS) * g + b


def _rope_lanes(z, tab):
    return (z * tab[:, :LANES]
            + pltpu.roll(z, QK_ROPE // 2, axis=1) * tab[:, LANES:2 * LANES]
            + pltpu.roll(z, LANES - QK_ROPE // 2, axis=1) * tab[:, 2 * LANES:])


def _mm_kernel(*refs, n_extra, n_out, epilogue):
    a_ref, w_ref = refs[0], refs[1]
    extra = refs[2:2 + n_extra]
    outs = refs[2 + n_extra:2 + n_extra + n_out]
    acc = _dot(a_ref[...], w_ref[...])
    res = epilogue(acc, *[e[...] for e in extra])
    for o, r in zip(outs, res):
        o[...] = r.astype(o.dtype)


def _matmul(a, w, epilogue, out_dtypes, extras=(), tm=1024, tn=1024, name="mm"):
    m, k = a.shape
    n = w.shape[1]
    tm = _tile(m, tm)
    tn = _tile(n, tn)
    in_specs = [pl.BlockSpec((tm, k), lambda i, j: (i, 0)),
                pl.BlockSpec((k, tn), lambda i, j: (0, j))]
    args = [a, w]
    for kind, arr in extras:
        if kind == "col":
            in_specs.append(pl.BlockSpec((1, tn), lambda i, j: (0, j)))
        else:
            in_specs.append(pl.BlockSpec((tm, tn), lambda i, j: (i, j)))
        args.append(arr)
    out_shape = [jax.ShapeDtypeStruct((m, n), dt) for dt in out_dtypes]
    out_specs = [pl.BlockSpec((tm, tn), lambda i, j: (i, j)) for _ in out_dtypes]
    kern = functools.partial(_mm_kernel, n_extra=len(extras), n_out=len(out_dtypes), epilogue=epilogue)
    return pl.pallas_call(
        kern, grid=(m // tm, n // tn), in_specs=in_specs, out_specs=out_specs, out_shape=out_shape,
        compiler_params=_params("parallel", "arbitrary"), name=name)(*args)


def _ep_identity(acc):
    return (acc,)


def _ep_gelu(acc):
    return (0.5 * acc * (1.0 + lax.erf(acc * (2.0 ** -0.5))),)


def _ep_sigmoid_bias(acc, b):
    return (jax.nn.sigmoid(acc + b),)


def _ep_scale(acc, *, scale):
    return (acc * scale,)


def _latent_kernel(x_ref, w_ref, qg_ref, kvg_ref, tab_ref, cq_ref, ckv_ref, kr_ref):
    z = _dot(x_ref[...], w_ref[...])

    def rms(t, g):
        return t * lax.rsqrt(jnp.mean(t * t, axis=-1, keepdims=True) + RMS_EPS) * g

    cq_ref[...] = rms(z[:, :OFF_KV], qg_ref[...]).astype(BF16)
    ckv_ref[...] = rms(z[:, OFF_KV:OFF_KR], kvg_ref[...]).astype(BF16)
    kr_ref[...] = _rope_lanes(z[:, OFF_KR:], tab_ref[...]).astype(BF16)


def _latent(xb, w_lat, qg, kvg, tab):
    m, k = xb.shape
    tm = _tile(m, 512)
    return pl.pallas_call(
        _latent_kernel, grid=(m // tm,),
        in_specs=[pl.BlockSpec((tm, k), lambda i: (i, 0)),
                  pl.BlockSpec((k, LAT_PAD), lambda i: (0, 0)),
                  pl.BlockSpec((1, Q_LORA), lambda i: (0, 0)),
                  pl.BlockSpec((1, KV_LORA), lambda i: (0, 0)),
                  pl.BlockSpec((tm, 3 * LANES), lambda i: (i, 0))],
        out_specs=[pl.BlockSpec((tm, Q_LORA), lambda i: (i, 0)),
                   pl.BlockSpec((tm, KV_LORA), lambda i: (i, 0)),
                   pl.BlockSpec((tm, LANES), lambda i: (i, 0))],
        out_shape=[jax.ShapeDtypeStruct((m, Q_LORA), BF16),
                   jax.ShapeDtypeStruct((m, KV_LORA), BF16),
                   jax.ShapeDtypeStruct((m, LANES), BF16)],
        compiler_params=_params("parallel"), name="latent")(xb, w_lat, qg, kvg, tab)


def _qproj_kernel(c_ref, w_ref, tab_ref, q_ref, *, heads_per_tile):
    acc = _dot(c_ref[...], w_ref[...])
    tab = tab_ref[...]
    for h in range(heads_per_tile):
        lo = h * Q_HEAD_PAD
        q_ref[:, lo:lo + LANES] = (acc[:, lo:lo + LANES] * Q_PRESCALE).astype(BF16)
        q_ref[:, lo + LANES:lo + Q_HEAD_PAD] = (
            _rope_lanes(acc[:, lo + LANES:lo + Q_HEAD_PAD], tab) * Q_PRESCALE).astype(BF16)


def _qproj(cq, wq, tab):
    m, k = cq.shape
    n = wq.shape[1]
    tm = _tile(m, 1024)
    tn = 4 * Q_HEAD_PAD
    return pl.pallas_call(
        functools.partial(_qproj_kernel, heads_per_tile=tn // Q_HEAD_PAD),
        grid=(m // tm, n // tn),
        in_specs=[pl.BlockSpec((tm, k), lambda i, j: (i, 0)),
                  pl.BlockSpec((k, tn), lambda i, j: (0, j)),
                  pl.BlockSpec((tm, 3 * LANES), lambda i, j: (i, 0))],
        out_specs=pl.BlockSpec((tm, tn), lambda i, j: (i, j)),
        out_shape=jax.ShapeDtypeStruct((m, n), BF16),
        compiler_params=_params("parallel", "arbitrary"), name="qproj")(cq, wq, tab)


MLA_HEADS_PER_STEP = 4
MLA_SCORE_SLOTS = 2
MLA_KEY_CHUNK = 512
MLA_Q_TILE = 512
MLA_SHORT_SEQ = 2048
MLA_Q_TILE_SHORT = 1024


def _mla_kernel(q_ref, kv_ref, kr_ref, o_ref, kcat_ref, vt_ref, s_ref, *, kc):
    seq = kv_ref.shape[0]
    tq = q_ref.shape[0]
    nck = seq // kc
    nt = (((1,), (1,)), ((), ()))

    @pl.when(pl.program_id(2) == 0)
    def _():
        for h in range(MLA_HEADS_PER_STEP):
            kcat_ref[h, :, :LANES] = kv_ref[:, h * Q_HEAD_PAD:h * Q_HEAD_PAD + LANES]
            kcat_ref[h, :, LANES:] = kr_ref[...]
            for c in range(nck):
                v = kv_ref[c * kc:(c + 1) * kc, h * Q_HEAD_PAD + LANES:(h + 1) * Q_HEAD_PAD]
                vt_ref[h, :, c * kc:(c + 1) * kc] = v.astype(F32).T.astype(BF16)

    for h in range(MLA_HEADS_PER_STEP):
        q = q_ref[:, h * Q_HEAD_PAD:(h + 1) * Q_HEAD_PAD]
        slot = h % MLA_SCORE_SLOTS
        m8 = jnp.full((SUBLANES, tq), -jnp.inf, F32)
        for c in range(nck):
            s_c = lax.dot_general(kcat_ref[h, c * kc:(c + 1) * kc, :], q, nt, preferred_element_type=F32)
            s_ref[slot, c * kc:(c + 1) * kc, :] = s_c
            m8 = jnp.maximum(m8, jnp.max(s_c.reshape(kc // SUBLANES, SUBLANES, tq), axis=0))
        m = jnp.max(m8, axis=0, keepdims=True)
        l8 = jnp.zeros((SUBLANES, tq), F32)
        o_t = jnp.zeros((V_HEAD, tq), F32)
        for c in range(nck):
            p = jnp.exp2(s_ref[slot, c * kc:(c + 1) * kc, :] - m)
            l8 = l8 + jnp.sum(p.reshape(kc // SUBLANES, SUBLANES, tq), axis=0)
            o_t = o_t + _dot(vt_ref[h, :, c * kc:(c + 1) * kc], p.astype(BF16))
        l = jnp.sum(l8, axis=0, keepdims=True)
        o_ref[:, h * V_HEAD:(h + 1) * V_HEAD] = (o_t / l).T.astype(BF16)


def _mla_attention(q, kv, kr, row0, nb, s):
    tq = _tile(s, MLA_Q_TILE_SHORT if s <= MLA_SHORT_SEQ else MLA_Q_TILE)
    kc = _tile(s, MLA_KEY_CHUNK)
    nq = s // tq
    qb0 = row0 // tq
    sb0 = row0 // s
    hp = MLA_HEADS_PER_STEP
    return pl.pallas_call(
        functools.partial(_mla_kernel, kc=kc), grid=(nb, MLA_HEADS // hp, nq),
        in_specs=[pl.BlockSpec((tq, hp * Q_HEAD_PAD), lambda b, h, i: (qb0 + b * nq + i, h)),
                  pl.BlockSpec((s, hp * Q_HEAD_PAD), lambda b, h, i: (sb0 + b, h)),
                  pl.BlockSpec((s, LANES), lambda b, h, i: (sb0 + b, 0))],
        out_specs=pl.BlockSpec((tq, hp * V_HEAD), lambda b, h, i: (b * nq + i, h)),
        out_shape=jax.ShapeDtypeStruct((nb * s, MLA_HEADS * V_HEAD), BF16),
        scratch_shapes=[pltpu.VMEM((hp, s, Q_HEAD_PAD), BF16), pltpu.VMEM((hp, V_HEAD, s), BF16),
                        pltpu.VMEM((MLA_SCORE_SLOTS, s, tq), F32)],
        compiler_params=_params("parallel", "parallel", "arbitrary"), name="mla_attn")(q, kv, kr)


def _sgu_kernel(u_ref, v_ref, g_ref, b_ref, ws_ref, bs_ref, o_ref, *, chunks):
    v = _layer_norm(v_ref[...].astype(F32), g_ref[...], b_ref[...]).astype(BF16)
    gw = SGU_WIDTH // SGU_GROUPS
    for c in range(chunks):
        r0 = c * SGU_CHUNK
        for g in range(SGU_GROUPS):
            sg = _dot(ws_ref[g], v[r0:r0 + SGU_CHUNK, g * gw:(g + 1) * gw])
            sg = sg + bs_ref[:, g * gw:(g + 1) * gw]
            u = u_ref[r0:r0 + SGU_CHUNK, g * gw:(g + 1) * gw].astype(F32)
            o_ref[r0:r0 + SGU_CHUNK, g * gw:(g + 1) * gw] = (u * sg).astype(BF16)


def _sgu(zs, ln_g, ln_b, ws, bs_full):
    m = zs.shape[0]
    tc = _tile(m, 2 * SGU_CHUNK)
    return pl.pallas_call(
        functools.partial(_sgu_kernel, chunks=tc // SGU_CHUNK), grid=(m // tc,),
        in_specs=[pl.BlockSpec((tc, SGU_WIDTH), lambda i: (i, 0)),
                  pl.BlockSpec((tc, SGU_WIDTH), lambda i: (i, 1)),
                  pl.BlockSpec((1, SGU_WIDTH), lambda i: (0, 0)),
                  pl.BlockSpec((1, SGU_WIDTH), lambda i: (0, 0)),
                  pl.BlockSpec((SGU_GROUPS, SGU_CHUNK, SGU_CHUNK), lambda i: (0, 0, 0)),
                  pl.BlockSpec((SGU_CHUNK, SGU_WIDTH), lambda i: (0, 0))],
        out_specs=pl.BlockSpec((tc, SGU_WIDTH), lambda i: (i, 0)),
        out_shape=jax.ShapeDtypeStruct((m, SGU_WIDTH), BF16),
        compiler_params=_params("parallel"), name="sgu")(zs, zs, ln_g, ln_b, ws, bs_full)


def _chan_dft_kernel(z_ref, w_ref, c_ref, s_ref):
    acc = _dot(z_ref[...], w_ref[...])
    c_ref[...] = acc[:, :FNET_GW].astype(BF16)
    s_ref[...] = acc[:, FNET_GW:].astype(BF16)


def _chan_dft(zf, w_cs):
    m = zf.shape[0]
    tm = _tile(m, 1024)
    blk = pl.BlockSpec((tm, FNET_GW), lambda i, g: (i, g))
    return pl.pallas_call(
        _chan_dft_kernel, grid=(m // tm, FNET_GROUPS),
        in_specs=[blk, pl.BlockSpec((FNET_GW, 2 * FNET_GW), lambda i, g: (0, 0))],
        out_specs=[blk, blk],
        out_shape=[jax.ShapeDtypeStruct((m, FNET_WIDTH), BF16)] * 2,
        compiler_params=_params("parallel", "arbitrary"), name="chan_dft")(zf, w_cs)


def _seq_dft_kernel(cs_ref, ss_ref, xc_ref, xs_ref, y_ref):
    y = _dot(cs_ref[...], xc_ref[...]) - _dot(ss_ref[...], xs_ref[...])
    y_ref[...] = y.astype(BF16)


def _seq_dft(cs, ss, xc, xs, row0, nb, s):
    tm = _tile(s, 512)
    tn = 512
    ni = s // tm
    sb0 = row0 // s
    return pl.pallas_call(
        _seq_dft_kernel, grid=(nb, FNET_WIDTH // tn, ni),
        in_specs=[pl.BlockSpec((tm, s), lambda b, j, i: (i, 0)),
                  pl.BlockSpec((tm, s), lambda b, j, i: (i, 0)),
                  pl.BlockSpec((s, tn), lambda b, j, i: (sb0 + b, j)),
                  pl.BlockSpec((s, tn), lambda b, j, i: (sb0 + b, j))],
        out_specs=pl.BlockSpec((tm, tn), lambda b, j, i: (b * ni + i, j)),
        out_shape=jax.ShapeDtypeStruct((nb * s, FNET_WIDTH), BF16),
        compiler_params=_params("parallel", "parallel", "arbitrary"), name="seq_dft")(cs, ss, xc, xs)


DFT_SPLIT = 64


def _dft_tables(n):
    k = np.arange(n, dtype=np.int64)
    j1 = np.arange(n // DFT_SPLIT, dtype=np.int64)
    j0 = np.arange(DFT_SPLIT, dtype=np.int64)
    w = 2.0 * math.pi / n
    ang_a = ((j1[:, None] * k[None, :] * DFT_SPLIT) % n) * w
    ang_b = ((j0[:, None] * k[None, :]) % n) * w
    ca, sa = (jnp.asarray(t, F32)[:, None, :] for t in (np.cos(ang_a), np.sin(ang_a)))
    cb, sb = (jnp.asarray(t, F32)[None, :, :] for t in (np.cos(ang_b), np.sin(ang_b)))
    sc = n ** -0.5
    cos = ((ca * cb - sa * sb) * sc).reshape(n, n)
    sin = ((sa * cb + ca * sb) * sc).reshape(n, n)
    return cos.astype(BF16), sin.astype(BF16)


def _branch_kernel(a0, a1, a2, w0, w1, w2, x, wg0, wg1, wg2, bg0, bg1, bg2, m_ref):
    xv = x[...]
    g0 = jax.nn.sigmoid(_dot(xv, wg0[...]) + bg0[...])
    g1 = jax.nn.sigmoid(_dot(xv, wg1[...]) + bg1[...])
    g2 = jax.nn.sigmoid(_dot(xv, wg2[...]) + bg2[...])
    m = g0 * _dot(a0[...], w0[...]) + g1 * _dot(a1[...], w1[...]) + g2 * _dot(a2[...], w2[...])
    m_ref[...] = m.astype(BF16)


def _branch_merge(a0, a1, a2, w0, w1, w2, xb, w_gate, b_gate):
    m = a0.shape[0]
    tm = _tile(m, 512)
    tn = 512
    nb = D_MODEL // tn
    a_spec = pl.BlockSpec((tm, D_MODEL), lambda i, j: (i, 0))
    w_spec = pl.BlockSpec((D_MODEL, tn), lambda i, j: (0, j))
    wg_specs = [pl.BlockSpec((D_MODEL, tn), functools.partial(lambda i, j, r: (0, r * nb + j), r=r))
                for r in range(N_BRANCH)]
    bg_specs = [pl.BlockSpec((1, tn), functools.partial(lambda i, j, r: (0, r * nb + j), r=r))
                for r in range(N_BRANCH)]
    return pl.pallas_call(
        _branch_kernel, grid=(m // tm, nb),
        in_specs=[a_spec] * 3 + [w_spec] * 3 + [a_spec] + wg_specs + bg_specs,
        out_specs=pl.BlockSpec((tm, tn), lambda i, j: (i, j)),
        out_shape=jax.ShapeDtypeStruct((m, D_MODEL), BF16),
        compiler_params=_params("parallel", "arbitrary"), name="branch_merge")(
            a0, a1, a2, w0, w1, w2, xb, w_gate, w_gate, w_gate, b_gate, b_gate, b_gate)


def _proj_ln_kernel(a_ref, w_ref, x_ref, g_ref, b_ref, xo_ref, xb_ref, *, alpha):
    y = alpha * x_ref[...] + _dot(a_ref[...], w_ref[...])
    xn = _layer_norm(y, g_ref[...], b_ref[...])
    xo_ref[...] = xn
    xb_ref[...] = xn.astype(BF16)


def _proj_ln(a, w, x, g, b, alpha):
    m = a.shape[0]
    tm = _tile(m, 512)
    row = pl.BlockSpec((tm, D_MODEL), lambda i: (i, 0))
    vec = pl.BlockSpec((1, D_MODEL), lambda i: (0, 0))
    return pl.pallas_call(
        functools.partial(_proj_ln_kernel, alpha=alpha), grid=(m // tm,),
        in_specs=[row, pl.BlockSpec((D_MODEL, D_MODEL), lambda i: (0, 0)), row, vec, vec],
        out_specs=[row, row],
        out_shape=[jax.ShapeDtypeStruct((m, D_MODEL), F32), jax.ShapeDtypeStruct((m, D_MODEL), BF16)],
        compiler_params=_params("parallel"), name="proj_ln")(a, w, x, g, b)


def _xattn_kernel(q_ref, k_ref, v_ref, o_ref):
    for h in range(X_HEADS):
        sl = slice(h * X_HEAD_DIM, (h + 1) * X_HEAD_DIM)
        s = lax.dot_general(q_ref[:, sl], k_ref[:, sl], (((1,), (1,)), ((), ())),
                            preferred_element_type=F32)
        p = jnp.exp(s - jnp.max(s, axis=-1, keepdims=True))
        l = jnp.sum(p, axis=-1, keepdims=True)
        o_ref[:, sl] = (_dot(p.astype(BF16), v_ref[:, sl]) / l).astype(BF16)


def _xattn(q, km, vm, mem_len, groups):
    m = q.shape[0]
    tq = _tile(min(s for _, s in groups), 512)
    bounds = []
    t0 = r0 = 0
    for rows, s in groups:
        bounds.append((t0, s // tq, r0))
        t0 += rows // tq
        r0 += rows // s

    def mem_block(i):
        blk = jnp.int32(0)
        for first, per, req0 in bounds:
            blk = jnp.where(i >= first, req0 + (i - first) // per, blk)
        return blk

    row = pl.BlockSpec((tq, D_MODEL), lambda i: (i, 0))
    mem = pl.BlockSpec((mem_len, D_MODEL), lambda i: (mem_block(i), 0))
    return pl.pallas_call(
        _xattn_kernel, grid=(m // tq,), in_specs=[row, mem, mem], out_specs=row,
        out_shape=jax.ShapeDtypeStruct((m, D_MODEL), BF16),
        compiler_params=_params("parallel"), name="xattn")(q, km, vm)


def _router_kernel(x_ref, whi_ref, wlo_ref, b_ref, ir_ref, p_ref, cnt_ref, carry_ref):
    i = pl.program_id(0)

    @pl.when(i == 0)
    def _():
        carry_ref[...] = jnp.zeros_like(carry_ref)

    tm = x_ref.shape[0]
    x = x_ref[...]
    x_hi = x.astype(BF16)
    x_lo = (x - x_hi.astype(F32)).astype(BF16)
    logits = (_dot(x_hi, whi_ref[...]) + _dot(x_hi, wlo_ref[...]) + _dot(x_lo, whi_ref[...])
              + b_ref[...])
    lane = lax.broadcasted_iota(jnp.int32, (tm, LANES), 1)
    tops, idxs = [], []
    cur = logits
    for _ in range(TOP_K):
        mx = jnp.max(cur, axis=-1, keepdims=True)
        ix = jnp.min(jnp.where(cur == mx, lane, LANES), axis=-1, keepdims=True)
        tops.append(mx)
        idxs.append(ix)
        cur = jnp.where(lane == ix, NEG_BIG * 2.0, cur)
    exps = [jnp.exp(t - tops[0]) for t in tops]
    denom = exps[0] + exps[1] + exps[2] + exps[3]

    onehot = jnp.zeros((tm, LANES), F32)
    for ix in idxs:
        onehot = onehot + (lane == ix).astype(F32)
    r = lax.broadcasted_iota(jnp.int32, (tm, tm), 0)
    c = lax.broadcasted_iota(jnp.int32, (tm, tm), 1)
    lower = (c < r).astype(BF16)
    before = _dot(lower, onehot.astype(BF16)) + carry_ref[...]

    ir = jnp.zeros((tm, LANES), jnp.int32)
    pr = jnp.zeros((tm, LANES), F32)
    for k in range(TOP_K):
        rank = jnp.sum(jnp.where(lane == idxs[k], before, 0.0), axis=-1, keepdims=True)
        ir = jnp.where(lane == k, idxs[k], ir)
        ir = jnp.where(lane == TOP_K + k, rank.astype(jnp.int32), ir)
        pr = jnp.where(lane == k, exps[k] / denom, pr)
    ir_ref[...] = ir
    p_ref[...] = pr
    carry_ref[...] = carry_ref[...] + jnp.sum(onehot, axis=0, keepdims=True)
    cnt_ref[...] = carry_ref[...]


def _router(x, w_pad, b_pad):
    m = x.shape[0]
    tm = _tile(m, 512)
    w_hi = w_pad.astype(BF16)
    w_lo = (w_pad - w_hi.astype(F32)).astype(BF16)
    return pl.pallas_call(
        _router_kernel, grid=(m // tm,),
        in_specs=[pl.BlockSpec((tm, D_MODEL), lambda i: (i, 0)),
                  pl.BlockSpec((D_MODEL, LANES), lambda i: (0, 0)),
                  pl.BlockSpec((D_MODEL, LANES), lambda i: (0, 0)),
                  pl.BlockSpec((1, LANES), lambda i: (0, 0))],
        out_specs=[pl.BlockSpec((tm, LANES), lambda i: (i, 0)),
                   pl.BlockSpec((tm, LANES), lambda i: (i, 0)),
                   pl.BlockSpec((1, LANES), lambda i: (0, 0))],
        out_shape=[jax.ShapeDtypeStruct((m, LANES), jnp.int32),
                   jax.ShapeDtypeStruct((m, LANES), F32),
                   jax.ShapeDtypeStruct((1, LANES), F32)],
        scratch_shapes=[pltpu.VMEM((1, LANES), F32)],
        compiler_params=_params("arbitrary"), name="router")(x, w_hi, w_lo, b_pad)


def _moe_kernel(te_ref, nu_ref, idx_cur, idx_nxt, x_hbm, wg_ref, wu_ref, bg_ref, bu_ref, wd_ref, bd_ref,
                out_ref, xbuf, xb_ref, sem, *, tm, nf, sub):
    j = pl.program_id(0)
    f = pl.program_id(1)
    n_used = nu_ref[0]

    def start_gather(idx_ref, slot):
        def body(g, carry):
            for u in range(SUBLANES):
                t = idx_ref[0, 0, g * SUBLANES + u]
                pltpu.make_async_copy(
                    x_hbm.at[lax.shift_right_logical(t, 3), pl.ds(lax.bitwise_and(t, SUBLANES - 1), 1), :],
                    xbuf.at[slot, g, pl.ds(u, 1), :], sem.at[slot]).start(priority=GATHER_DMA_PRIORITY)
            return carry
        lax.fori_loop(0, tm // SUBLANES, body, 0)

    @pl.when(jnp.logical_and(jnp.logical_and(j == 0, f == 0), n_used > 0))
    def _():
        start_gather(idx_cur, 0)

    @pl.when(j < n_used)
    def _():
        slot = j % 2

        @pl.when(f == 0)
        def _():
            pltpu.make_async_copy(x_hbm.at[pl.ds(0, tm // SUBLANES)], xbuf.at[slot], sem.at[slot]).wait()
            xb_ref[...] = xbuf[slot].reshape(tm, D_MODEL).astype(BF16)

        x = xb_ref[...]
        part = None
        for c in range(wg_ref.shape[3] // sub):
            sl = slice(c * sub, (c + 1) * sub)
            gate = jnp.minimum(_dot(x, wg_ref[0, 0, :, sl]) + bg_ref[0, 0, :, sl], SWIGLU_LIMIT)
            up = jnp.clip(_dot(x, wu_ref[0, 0, :, sl]) + bu_ref[0, 0, :, sl], -SWIGLU_LIMIT, SWIGLU_LIMIT)
            act = (up + 1.0) * gate * jax.nn.sigmoid(SWIGLU_ALPHA * gate)
            d = _dot(act.astype(BF16), wd_ref[0, 0, sl, :])
            part = d if part is None else part + d

        @pl.when(f == 0)
        def _():
            out_ref[...] = part + bd_ref[0, 0]

        @pl.when(f > 0)
        def _():
            out_ref[...] += part

    @pl.when(jnp.logical_and(f == 0, j + 1 < n_used))
    def _():
        start_gather(idx_nxt, (j + 1) % 2)

    @pl.when(jnp.logical_and(j >= n_used, f == nf - 1))
    def _():
        out_ref[...] = jnp.zeros_like(out_ref)


MOE_FF_BLOCK = 1024
MOE_FF_SUB = 512


def _moe_experts(layer, tile_expert, n_used, src_tok, x, w_gu, b_gu, w_down, b_down, tm):
    n_tiles = tile_expert.shape[0]
    tf = MOE_FF_BLOCK
    nf = D_FF // tf
    idx3 = src_tok.reshape(n_tiles, 1, tm)

    def fchunk(j, f, nu):
        snake = jnp.where(j % 2 == 0, f, nf - 1 - f)
        last = jnp.where((nu[0] - 1) % 2 == 0, nf - 1, 0)
        return jnp.where(j < nu[0], snake, last)

    grid_spec = pltpu.PrefetchScalarGridSpec(
        num_scalar_prefetch=2, grid=(n_tiles, nf),
        in_specs=[
            pl.BlockSpec((1, 1, tm), lambda j, f, te, nu: (j, 0, 0), memory_space=pltpu.SMEM),
            pl.BlockSpec((1, 1, tm), lambda j, f, te, nu: (jnp.minimum(j + 1, n_tiles - 1), 0, 0),
                         memory_space=pltpu.SMEM),
            pl.BlockSpec(memory_space=pl.ANY),
            pl.BlockSpec((1, 1, D_MODEL, tf), lambda j, f, te, nu: (layer, te[j], 0, fchunk(j, f, nu))),
            pl.BlockSpec((1, 1, D_MODEL, tf), lambda j, f, te, nu: (layer, te[j], 0, nf + fchunk(j, f, nu))),
            pl.BlockSpec((1, 1, 1, tf), lambda j, f, te, nu: (layer, te[j], 0, fchunk(j, f, nu))),
            pl.BlockSpec((1, 1, 1, tf), lambda j, f, te, nu: (layer, te[j], 0, nf + fchunk(j, f, nu))),
            pl.BlockSpec((1, 1, tf, D_MODEL), lambda j, f, te, nu: (layer, te[j], fchunk(j, f, nu), 0)),
            pl.BlockSpec((1, 1, 1, D_MODEL), lambda j, f, te, nu: (layer, te[j], 0, 0)),
        ],
        out_specs=pl.BlockSpec((tm, D_MODEL), lambda j, f, te, nu: (j, 0)),
        scratch_shapes=[pltpu.VMEM((2, tm // SUBLANES, SUBLANES, D_MODEL), F32),
                        pltpu.VMEM((tm, D_MODEL), BF16),
                        pltpu.SemaphoreType.DMA((2,))])
    return pl.pallas_call(
        functools.partial(_moe_kernel, tm=tm, nf=nf, sub=MOE_FF_SUB), grid_spec=grid_spec,
        out_shape=jax.ShapeDtypeStruct((n_tiles * tm, D_MODEL), F32),
        compiler_params=_params("arbitrary", "arbitrary"), name="moe_experts")(
            tile_expert, n_used, idx3, idx3, x.reshape(-1, SUBLANES, D_MODEL),
            w_gu, w_gu, b_gu, b_gu, w_down, b_down)


def _combine_kernel(pos_cur, pos_nxt, ys_hbm, p_ref, x_ref, g_ref, b_ref, xo_ref, xb_ref, buf, sem,
                    *, tm, alpha, first_tiles):
    i = pl.program_id(0)
    n = pl.num_programs(0)

    def start_gather(pos_ref, slot):
        def body(g, carry):
            for u in range(SUBLANES):
                for k in range(TOP_K):
                    t = pos_ref[0, 0, (g * SUBLANES + u) * TOP_K + k]
                    pltpu.make_async_copy(
                        ys_hbm.at[lax.shift_right_logical(t, 3), pl.ds(lax.bitwise_and(t, SUBLANES - 1), 1), :],
                        buf.at[slot, k, g, pl.ds(u, 1), :], sem.at[slot]).start(priority=k % 2)
            return carry
        lax.fori_loop(0, tm // SUBLANES, body, 0)

    @pl.when(i == 0)
    def _():
        start_gather(pos_cur, 0)

    @pl.when(i + 1 < n)
    def _():
        start_gather(pos_nxt, (i + 1) % 2)

    slot = i % 2
    for k in range(TOP_K):
        pltpu.make_async_copy(ys_hbm.at[pl.ds(0, tm // SUBLANES)], buf.at[slot, k], sem.at[slot]).wait()
    p = p_ref[...]
    y = alpha * x_ref[...]
    for k in range(TOP_K):
        y = y + p[:, k:k + 1] * buf[slot, k].reshape(tm, D_MODEL)
    xn = _layer_norm(y, g_ref[...], b_ref[...])
    if first_tiles is None:
        xo_ref[...] = xn
        xb_ref[...] = xn.astype(BF16)
    else:
        @pl.when(i < first_tiles)
        def _():
            xo_ref[...] = xn

        @pl.when(i >= first_tiles)
        def _():
            xb_ref[...] = xn


def _combine_ln(pos, ys, probs, x, g, b, alpha, split_rows=None):
    m = x.shape[0]
    tm = _tile(m, 256) if split_rows is None else _tile(math.gcd(split_rows, m - split_rows), 256)
    nt = m // tm
    pos3 = pos.reshape(nt, 1, tm * TOP_K)
    row = pl.BlockSpec((tm, D_MODEL), lambda i: (i, 0))
    vec = pl.BlockSpec((1, D_MODEL), lambda i: (0, 0))
    if split_rows is None:
        first_tiles = None
        out_specs = [row, row]
        out_shape = [jax.ShapeDtypeStruct((m, D_MODEL), F32), jax.ShapeDtypeStruct((m, D_MODEL), BF16)]
    else:
        first_tiles = split_rows // tm
        out_specs = [pl.BlockSpec((tm, D_MODEL), lambda i: (jnp.minimum(i, first_tiles - 1), 0)),
                     pl.BlockSpec((tm, D_MODEL), lambda i: (jnp.maximum(i - first_tiles, 0), 0))]
        out_shape = [jax.ShapeDtypeStruct((split_rows, D_MODEL), F32),
                     jax.ShapeDtypeStruct((m - split_rows, D_MODEL), F32)]
    return pl.pallas_call(
        functools.partial(_combine_kernel, tm=tm, alpha=alpha, first_tiles=first_tiles), grid=(nt,),
        in_specs=[pl.BlockSpec((1, 1, tm * TOP_K), lambda i: (i, 0, 0), memory_space=pltpu.SMEM),
                  pl.BlockSpec((1, 1, tm * TOP_K), lambda i: (jnp.minimum(i + 1, nt - 1), 0, 0),
                               memory_space=pltpu.SMEM),
                  pl.BlockSpec(memory_space=pl.ANY),
                  pl.BlockSpec((tm, LANES), lambda i: (i, 0)),
                  row, vec, vec],
        out_specs=out_specs, out_shape=out_shape,
        scratch_shapes=[pltpu.VMEM((2, TOP_K, tm // SUBLANES, SUBLANES, D_MODEL), F32),
                        pltpu.SemaphoreType.DMA((2,))],
        compiler_params=_params("arbitrary"), name="combine_ln")(
            pos3, pos3, ys.reshape(-1, SUBLANES, D_MODEL), probs, x, g, b)


def _routing_plan(ir, counts, tm):
    m = ir.shape[0]
    n_tiles = (m * TOP_K) // tm + N_EXPERTS
    idx = ir[:, :TOP_K]
    rank = ir[:, TOP_K:2 * TOP_K]
    cnt = counts[0, :N_EXPERTS].astype(jnp.int32)
    tiles_per = (cnt + tm - 1) // tm
    tile_end = jnp.cumsum(tiles_per)
    tile_start = tile_end - tiles_per
    n_used = tile_end[-1]
    experts = jnp.arange(N_EXPERTS, dtype=jnp.int32)
    start_of = jnp.sum(jnp.where(idx[:, :, None] == experts, tile_start, 0), axis=-1)
    pos = start_of * tm + rank
    tile_ids = jnp.minimum(jnp.arange(n_tiles, dtype=jnp.int32), n_used - 1)
    tile_expert = jnp.sum((tile_end[None, :] <= tile_ids[:, None]).astype(jnp.int32), axis=1)
    tile_expert = jnp.minimum(tile_expert, N_EXPERTS - 1)
    tok = jnp.broadcast_to(jnp.arange(m, dtype=jnp.int32)[:, None], (m, TOP_K))
    src_tok = jnp.zeros((n_tiles * tm,), jnp.int32).at[pos.reshape(-1)].set(tok.reshape(-1))
    return pos.astype(jnp.int32), tile_expert, n_used.reshape(1).astype(jnp.int32), src_tok


def _rope_table(groups):
    inv = 1.0 / (ROPE_THETA ** (np.arange(0, QK_ROPE, 2, dtype=np.float64) / QK_ROPE))
    parts = []
    for rows, s in groups:
        ang = np.arange(s, dtype=np.float64)[:, None] * inv[None, :]
        cos, sin = np.cos(ang), np.sin(ang)
        z = np.zeros_like(cos)
        base = np.concatenate([cos, cos, z, z, z, sin, z, z, -sin, z, z, z], axis=1).astype(np.float32)
        parts.append(jnp.tile(jnp.asarray(base), (rows // s, 1)))
    return jnp.concatenate(parts, axis=0)


def kernel(x_prompt, x_sample, mem_prompt, mem_sample, w_in, b_gate, mla_q_norm, mla_kv_norm, w_uq, w_ukv,
           w_mla_o, sgu_ln_g, sgu_ln_b, sgu_ws, sgu_bs, w_sgu_o, w_fnet_o, w_out, ln1_g, ln1_b, w_cq, w_ck,
           w_cv, w_co, ln2_g, ln2_b, w_router, b_router, w_gu, b_gu, w_down, b_down, ln3_g, ln3_b):
    depth = w_in.shape[0]
    alpha = (2 * depth) ** 0.25
    bp, sp, _ = x_prompt.shape
    bs_, ss, _ = x_sample.shape
    mem_len = mem_prompt.shape[1]
    groups = ((bp * sp, sp), (bs_ * ss, ss))
    assert (bp * sp) % ss == 0 and sp % SGU_CHUNK == 0 and ss % SGU_CHUNK == 0
    m = bp * sp + bs_ * ss

    x = jnp.concatenate([x_prompt.reshape(bp * sp, D_MODEL), x_sample.reshape(bs_ * ss, D_MODEL)], axis=0)
    xb = x.astype(BF16)
    memb = jnp.concatenate([mem_prompt.reshape(bp * mem_len, D_MODEL),
                            mem_sample.reshape(bs_ * mem_len, D_MODEL)], axis=0).astype(BF16)

    tab = _rope_table(groups)
    dft_c, dft_s = _dft_tables(FNET_GW)
    w_chan = jnp.concatenate([dft_c, dft_s], axis=1)
    seq_tabs = {s: _dft_tables(s) for s in {sp, ss}}
    moe_tm = _tile(m * TOP_K, 512)
    w_gu_b = w_gu.astype(BF16)
    w_down_b = w_down.astype(BF16)
    b_gu4 = b_gu.reshape(depth, N_EXPERTS, 1, 2 * D_FF)
    b_down4 = b_down.reshape(depth, N_EXPERTS, 1, D_MODEL)

    for l in range(depth):
        w_lat = jnp.pad(w_in[l, :, :OFF_SGU], ((0, 0), (0, LAT_PAD - OFF_SGU))).astype(BF16)
        w_sgu_in = w_in[l, :, OFF_SGU:OFF_FNET].astype(BF16)
        w_fnet_in = w_in[l, :, OFF_FNET:OFF_GATE].astype(BF16)
        w_gate_in = w_in[l, :, OFF_GATE:].astype(BF16)
        wq = jnp.pad(w_uq[l].reshape(Q_LORA, MLA_HEADS, QK_NOPE + QK_ROPE),
                     ((0, 0), (0, 0), (0, Q_HEAD_PAD - QK_NOPE - QK_ROPE))
                     ).reshape(Q_LORA, MLA_HEADS * Q_HEAD_PAD).astype(BF16)
        bs_full = jnp.repeat(sgu_bs[l].T, SGU_WIDTH // SGU_GROUPS, axis=1)

        cq, ckv, kr = _latent(xb, w_lat, mla_q_norm[l][None], mla_kv_norm[l][None], tab)
        q = _qproj(cq, wq, tab)
        (kv,) = _matmul(ckv, w_ukv[l].astype(BF16), _ep_identity, [BF16], name="kvproj")
        o_parts, row0 = [], 0
        for rows, s in groups:
            o_parts.append(_mla_attention(q, kv, kr, row0, rows // s, s))
            row0 += rows
        o_mla = jnp.concatenate(o_parts, axis=0)

        (zs,) = _matmul(xb, w_sgu_in, _ep_gelu, [BF16], name="sgu_in")
        us = _sgu(zs, sgu_ln_g[l][None], sgu_ln_b[l][None], sgu_ws[l].astype(BF16), bs_full)

        (zf,) = _matmul(xb, w_fnet_in, _ep_identity, [BF16], name="fnet_in")
        xc, xs = _chan_dft(zf, w_chan)
        f_parts, row0 = [], 0
        for rows, s in groups:
            cs, sn = seq_tabs[s]
            f_parts.append(_seq_dft(cs, sn, xc, xs, row0, rows // s, s))
            row0 += rows
        fy = jnp.concatenate(f_parts, axis=0)

        mrg = _branch_merge(o_mla, us, fy, w_mla_o[l].astype(BF16), w_sgu_o[l].astype(BF16),
                            w_fnet_o[l].astype(BF16), xb, w_gate_in, b_gate[l][None])
        x, xb = _proj_ln(mrg, w_out[l].astype(BF16), x, ln1_g[l][None], ln1_b[l][None], alpha)

        (qx,) = _matmul(xb, w_cq[l].astype(BF16), functools.partial(_ep_scale, scale=X_HEAD_DIM ** -0.5),
                        [BF16], name="xq")
        (km,) = _matmul(memb, w_ck[l].astype(BF16), _ep_identity, [BF16], name="xk")
        (vm,) = _matmul(memb, w_cv[l].astype(BF16), _ep_identity, [BF16], name="xv")
        ox = _xattn(qx, km, vm, mem_len, groups)
        x, xb = _proj_ln(ox, w_co[l].astype(BF16), x, ln2_g[l][None], ln2_b[l][None], alpha)

        w_r = jnp.pad(w_router[l], ((0, 0), (0, LANES - N_EXPERTS)))
        b_r = jnp.pad(b_router[l], (0, LANES - N_EXPERTS), constant_values=NEG_BIG)[None]
        ir, probs, counts = _router(x, w_r, b_r)
        pos, tile_expert, n_used, src_tok = _routing_plan(ir, counts, moe_tm)
        ys = _moe_experts(l, tile_expert, n_used, src_tok, x, w_gu_b, b_gu4, w_down_b, b_down4, moe_tm)
        if l + 1 < depth:
            x, xb = _combine_ln(pos, ys, probs, x, ln3_g[l][None], ln3_b[l][None], alpha)
        else:
            y_p, y_s = _combine_ln(pos, ys, probs, x, ln3_g[l][None], ln3_b[l][None], alpha,
                                   split_rows=bp * sp)
    return (y_p.reshape(bp, sp, D_MODEL), y_s.reshape(bs_, ss, D_MODEL))
```
